```python
import math
import jax, jax.numpy as jnp
from jax import lax
import numpy as np

D_MODEL = 1024
BATCH = 8
SEQ = 4096
DEPTH = 2

N_MIXERS = 2
N_A = (DEPTH + 1) // 2
N_B = DEPTH // 2

LRU_WIDTH = D_MODEL
N_LRU_BLOCKS = 4
LRU_BLOCK = LRU_WIDTH // N_LRU_BLOCKS
CONV_WIDTH = 4
LRU_C = 8.0
A_MIN = 0.9
A_MAX = 0.999

HEAD_DIM = 64
N_HEADS = D_MODEL // HEAD_DIM
Q_BLOCK = 128

D_FF = int(math.ceil(8 * D_MODEL / 3 / 256)) * 256

RMS_EPS = 1e-6

kernel_name = "hybrid_rglru_stickbreaking_swiglu"


def rms_norm(x, g):
    xf = x.astype(jnp.float32)
    y = xf * lax.rsqrt(jnp.mean(xf * xf, axis=-1, keepdims=True) + RMS_EPS)
    return (y * g.astype(jnp.float32)).astype(x.dtype)


def causal_depthwise_conv(x, w, b):
    c = x.shape[-1]
    y = lax.conv_general_dilated(
        x, w.reshape(CONV_WIDTH, 1, c).astype(x.dtype),
        window_strides=(1,), padding=[(CONV_WIDTH - 1, 0)],
        dimension_numbers=("NWC", "WIO", "NWC"), feature_group_count=c)
    return y + b


def _linear_recurrence_combine(e1, e2):
    a1, b1 = e1
    a2, b2 = e2
    return a1 * a2, a2 * b1 + b2


def rg_lru_block(x, w_in, b_in, conv_w, conv_b, gate_w, gate_b, lam, w_out, b_out):
    bsz, seq, _ = x.shape
    u = jnp.einsum("bsd,de->bse", x, w_in) + b_in
    gate_branch, rec = u[..., :LRU_WIDTH], u[..., LRU_WIDTH:]
    rec = causal_depthwise_conv(rec, conv_w, conv_b)
    rec_blk = rec.reshape(bsz, seq, N_LRU_BLOCKS, LRU_BLOCK)
    gl = jnp.einsum("bsnc,gncd->gbsnd", rec_blk, gate_w).reshape(2, bsz, seq, LRU_WIDTH)
    gl = gl.astype(jnp.float32) + gate_b.astype(jnp.float32)[:, None, None, :]
    r_gate = jax.nn.sigmoid(gl[0])
    i_gate = jax.nn.sigmoid(gl[1])
    log_a = -LRU_C * r_gate * jax.nn.softplus(-lam.astype(jnp.float32))
    a = jnp.exp(log_a)
    mult = jnp.sqrt(-jnp.expm1(2.0 * log_a))
    b = mult * (i_gate * rec.astype(jnp.float32))
    _, h = lax.associative_scan(_linear_recurrence_combine, (a, b), axis=1)
    y = jax.nn.gelu(gate_branch, approximate=True) * h.astype(x.dtype)
    return jnp.einsum("bse,ed->bsd", y, w_out) + b_out


def stick_breaking_attention(x, w_qkv, w_o):
    bsz, seq, _ = x.shape
    qkv = jnp.einsum("bsd,de->bse", x, w_qkv).reshape(bsz, seq, 3, N_HEADS, HEAD_DIM)
    q = jnp.transpose(qkv[:, :, 0], (0, 2, 1, 3))
    k = jnp.transpose(qkv[:, :, 1], (0, 2, 1, 3))
    v = jnp.transpose(qkv[:, :, 2], (0, 2, 1, 3))
    scale = 1.0 / math.sqrt(HEAD_DIM)
    outs = []
    for blk in range(seq // Q_BLOCK):
        t0 = blk * Q_BLOCK
        t1 = t0 + Q_BLOCK
        qb = q[:, :, t0:t1]
        kb = k[:, :, :t1]
        vb = v[:, :, :t1]
        z = jnp.einsum("bhqd,bhkd->bhqk", qb, kb).astype(jnp.float32) * scale
        t_idx = t0 + jnp.arange(Q_BLOCK)[:, None]
        s_idx = jnp.arange(t1)[None, :]
        strict = s_idx < t_idx
        log_fail = jnp.where(strict, jax.nn.log_sigmoid(-z), 0.0)
        suffix = lax.cumsum(log_fail, axis=3, reverse=True) - log_fail
        weights = jnp.where(strict, jnp.exp(jax.nn.log_sigmoid(z) + suffix), 0.0)
        outs.append(jnp.einsum("bhqk,bhkd->bhqd", weights.astype(vb.dtype), vb))
    o = jnp.concatenate(outs, axis=2)
    o = jnp.transpose(o, (0, 2, 1, 3)).reshape(bsz, seq, N_HEADS * HEAD_DIM)
    return jnp.einsum("bse,ed->bsd", o, w_o)


def swiglu_ffn(x, w_in, w_out):
    gu = jnp.einsum("bsd,df->bsf", x, w_in)
    h = jax.nn.silu(gu[..., :D_FF]) * gu[..., D_FF:]
    return jnp.einsum("bsf,fd->bsd", h, w_out)


def setup_inputs(seed: int = 0) -> dict:
    key = jax.random.key(seed)
    ks = jax.random.split(key, 20)
    f32 = jnp.float32

    def nrm(k, shape, fan_in):
        return jax.random.normal(k, shape, f32) * (fan_in ** -0.5)

    x = jax.random.normal(ks[0], (BATCH, SEQ, D_MODEL), f32)
    mix_norm = 1.0 + 0.05 * jax.random.normal(ks[1], (DEPTH, D_MODEL), f32)
    ffn_norm = 1.0 + 0.05 * jax.random.normal(ks[2], (DEPTH, D_MODEL), f32)
    final_norm = 1.0 + 0.05 * jax.random.normal(ks[3], (D_MODEL,), f32)

    lru_w_in = nrm(ks[4], (N_A, D_MODEL, 2 * LRU_WIDTH), D_MODEL)
    lru_b_in = 0.02 * jax.random.normal(ks[5], (N_A, 2 * LRU_WIDTH), f32)
    lru_conv_w = nrm(ks[6], (N_A, CONV_WIDTH, LRU_WIDTH), CONV_WIDTH)
    lru_conv_b = 0.02 * jax.random.normal(ks[7], (N_A, LRU_WIDTH), f32)
    lru_gate_w = nrm(ks[8], (N_A, 2, N_LRU_BLOCKS, LRU_BLOCK, LRU_BLOCK), LRU_BLOCK)
    lru_gate_b = 0.02 * jax.random.normal(ks[9], (N_A, 2, LRU_WIDTH), f32)
    u = jax.random.uniform(ks[10], (N_A, LRU_WIDTH), f32, A_MIN, A_MAX)
    a0 = u ** (1.0 / LRU_C)
    lru_lambda = jnp.log(a0) - jnp.log1p(-a0)
    lru_w_out = nrm(ks[11], (N_A, LRU_WIDTH, D_MODEL), LRU_WIDTH)
    lru_b_out = 0.02 * jax.random.normal(ks[12], (N_A, D_MODEL), f32)

    attn_w_qkv = nrm(ks[13], (N_B, D_MODEL, 3 * N_HEADS * HEAD_DIM), D_MODEL)
    attn_w_o = nrm(ks[14], (N_B, N_HEADS * HEAD_DIM, D_MODEL), N_HEADS * HEAD_DIM)

    ffn_w_in = nrm(ks[15], (DEPTH, D_MODEL, 2 * D_FF), D_MODEL)
    ffn_w_out = nrm(ks[16], (DEPTH, D_FF, D_MODEL), D_FF)

    return {
        "x": x, "mix_norm": mix_norm, "ffn_norm": ffn_norm, "final_norm": final_norm,
        "lru_w_in": lru_w_in, "lru_b_in": lru_b_in, "lru_conv_w": lru_conv_w,
        "lru_conv_b": lru_conv_b, "lru_gate_w": lru_gate_w, "lru_gate_b": lru_gate_b,
        "lru_lambda": lru_lambda, "lru_w_out": lru_w_out, "lru_b_out": lru_b_out,
        "attn_w_qkv": attn_w_qkv, "attn_w_o": attn_w_o,
        "ffn_w_in": ffn_w_in, "ffn_w_out": ffn_w_out,
    }


def reference(x, mix_norm, ffn_norm, final_norm, lru_w_in, lru_b_in, lru_conv_w,
              lru_conv_b, lru_gate_w, lru_gate_b, lru_lambda, lru_w_out, lru_b_out,
              attn_w_qkv, attn_w_o, ffn_w_in, ffn_w_out):
    h = x
    for layer in range(DEPTH):
        mixer = layer % N_MIXERS
        j = layer // N_MIXERS
        hn = rms_norm(h, mix_norm[layer])
        if mixer == 0:
            mixed = rg_lru_block(hn, lru_w_in[j], lru_b_in[j], lru_conv_w[j], lru_conv_b[j],
                                 lru_gate_w[j], lru_gate_b[j], lru_lambda[j],
                                 lru_w_out[j], lru_b_out[j])
        else:
            mixed = stick_breaking_attention(hn, attn_w_qkv[j], attn_w_o[j])
        h = h + mixed
        h = h + swiglu_ffn(rms_norm(h, ffn_norm[layer]), ffn_w_in[layer], ffn_w_out[layer])
    return rms_norm(h, final_norm)
```

```python
import functools
import math

import jax
import jax.numpy as jnp
from jax import lax
from jax.experimental import pallas as pl
from jax.experimental.pallas import tpu as pltpu

RMS_EPS = 1e-6
LRU_C = 8.0
HEAD_DIM = 64
N_LRU_BLOCKS = 4

V7X_LANES = 128
V7X_SUBLANES = 8
V7X_VMEM_BYTES = 64 * 1024 * 1024

KEY_TILE = 128
KEY_VREGS = KEY_TILE // V7X_SUBLANES
Q_TILE = 256
HEADS_PER_STEP = V7X_LANES // HEAD_DIM

BF16 = jnp.bfloat16
F32 = jnp.float32


def _vmem_limit(n_bytes):
    return int(min(n_bytes * 3 // 2 + (4 << 20), V7X_VMEM_BYTES - (4 << 20)))


def _rms_norm(x, g):
    return x * lax.rsqrt(jnp.mean(x * x, axis=-1, keepdims=True) + RMS_EPS) * g


def _dot(a, b):
    return jnp.dot(a, b, preferred_element_type=F32)


def _dot_nt(a, b):
    return lax.dot_general(a, b, (((1,), (1,)), ((), ())), preferred_element_type=F32)


def _resident(shape):
    zeros = (0,) * len(shape)
    return pl.BlockSpec(shape, lambda *_: zeros, pipeline_mode=pl.Buffered(1))


def _tile_scan(a, b):
    rows, w = a.shape
    a3 = a.reshape(rows // V7X_SUBLANES, V7X_SUBLANES, w)
    b3 = b.reshape(rows // V7X_SUBLANES, V7X_SUBLANES, w)
    sub = lax.broadcasted_iota(jnp.int32, (1, V7X_SUBLANES, w), 1)
    for k in (1, 2, 4):
        keep = sub >= k
        a_prev = jnp.where(keep, pltpu.roll(a3, k, 1), 1.0)
        b_prev = jnp.where(keep, pltpu.roll(b3, k, 1), 0.0)
        b3 = a3 * b_prev + b3
        a3 = a3 * a_prev
    return a3.reshape(rows, w), b3.reshape(rows, w)


def _lru_kernel(x_ref, g_ref, win_ref, bin_ref, cw_ref, cb_ref, gw_ref, gb_ref, lam_ref, wout_ref, bout_ref,
                o_ref, rec_buf, h_carry, *, conv_width):
    ts = x_ref.shape[1]
    w = wout_ref.shape[0]
    blk = w // N_LRU_BLOCKS
    pad = V7X_SUBLANES

    @pl.when(pl.program_id(1) == 0)
    def _():
        rec_buf[0:pad, :] = jnp.zeros((pad, w), F32)
        h_carry[...] = jnp.zeros_like(h_carry)

    x = x_ref[0]
    hn = _rms_norm(x, g_ref[...]).astype(BF16)
    u = _dot(hn, win_ref[...]) + bin_ref[...]
    gate_branch = u[:, :w]
    rec_buf[pad:pad + ts, :] = u[:, w:]

    conv = cb_ref[...]
    for k in range(conv_width):
        off = pad - (conv_width - 1) + k
        conv = conv + cw_ref[k:k + 1, :] * rec_buf[off:off + ts, :]
    rec_buf[0:pad, :] = rec_buf[ts:ts + pad, :]

    conv16 = conv.astype(BF16)
    gl = [_dot(conv16[:, n * blk:(n + 1) * blk], gw_ref[n]) for n in range(N_LRU_BLOCKS)]
    gl_r = jnp.concatenate([t[:, :blk] for t in gl], axis=1) + gb_ref[0:1, :]
    gl_i = jnp.concatenate([t[:, blk:] for t in gl], axis=1) + gb_ref[1:2, :]
    r_gate = jax.nn.sigmoid(gl_r)
    i_gate = jax.nn.sigmoid(gl_i)

    nlam = -lam_ref[...]
    softplus_nlam = jnp.maximum(nlam, 0.0) + jnp.log1p(jnp.exp(-jnp.abs(nlam)))
    log_a = (-LRU_C * softplus_nlam) * r_gate
    a = jnp.exp(log_a)
    mult = jnp.sqrt(1.0 - a * a)
    b = mult * (i_gate * conv)

    a_cum, b_cum = _tile_scan(a, b)
    hc = h_carry[...]
    hs = []
    for j in range(ts // V7X_SUBLANES):
        sl = slice(j * V7X_SUBLANES, (j + 1) * V7X_SUBLANES)
        h8 = a_cum[sl] * hc + b_cum[sl]
        hs.append(h8)
        hc = jnp.broadcast_to(h8[V7X_SUBLANES - 1:, :], h8.shape)
    h_carry[...] = hc
    h = jnp.concatenate(hs, axis=0)

    y = (jax.nn.gelu(gate_branch, approximate=True) * h).astype(BF16)
    o_ref[0] = _dot(y, wout_ref[...]) + bout_ref[...] + x


def _lru_block(x, g, w_in, b_in, conv_w, conv_b, gate_w, gate_b, lam, w_out, b_out, *, ts=256):
    bsz, seq, d = x.shape
    w = w_out.shape[0]
    conv_width = conv_w.shape[0]
    blk = w // N_LRU_BLOCKS
    gw = jnp.concatenate([gate_w[0], gate_w[1]], axis=-1).astype(BF16)
    row = lambda v: v.reshape(1, -1)
    est = (2 * 2 * ts * d * 4 + w_in.size * 2 + gw.size * 2 + w_out.size * 2
           + 8 * ts * w * 4)
    return pl.pallas_call(
        functools.partial(_lru_kernel, conv_width=conv_width),
        grid=(bsz, seq // ts),
        in_specs=[
            pl.BlockSpec((1, ts, d), lambda b, s: (b, s, 0)),
            _resident((1, d)),
            _resident((d, 2 * w)),
            _resident((1, 2 * w)),
            _resident((conv_width, w)),
            _resident((1, w)),
            _resident((N_LRU_BLOCKS, blk, 2 * blk)),
            _resident((2, w)),
            _resident((1, w)),
            _resident((w, d)),
            _resident((1, d)),
        ],
        out_specs=pl.BlockSpec((1, ts, d), lambda b, s: (b, s, 0)),
        out_shape=jax.ShapeDtypeStruct((bsz, seq, d), F32),
        scratch_shapes=[pltpu.VMEM((ts + 2 * V7X_SUBLANES, w), F32), pltpu.VMEM((V7X_SUBLANES, w), F32)],
        compiler_params=pltpu.CompilerParams(
            dimension_semantics=("parallel", "arbitrary"), vmem_limit_bytes=_vmem_limit(est)),
        name="lru",
    )(x, row(g), w_in.astype(BF16), row(b_in), conv_w, row(conv_b), gw, gate_b, row(lam),
      w_out.astype(BF16), row(b_out))


def _ffn_kernel(*refs, has_attn, final, fc):
    refs = list(refs)
    x_ref = refs.pop(0)
    o_ref, wo_ref = (refs.pop(0), refs.pop(0)) if has_attn else (None, None)
    g_ref, win_ref, wout_ref = refs.pop(0), refs.pop(0), refs.pop(0)
    fg_ref = refs.pop(0) if final else None
    out_ref = refs.pop(0)

    h = x_ref[0]
    if has_attn:
        h = h + _dot(o_ref[0], wo_ref[...])
    hn = _rms_norm(h, g_ref[...]).astype(BF16)
    acc = h
    for c in range(wout_ref.shape[0] // fc):
        gu = _dot(hn, win_ref[:, 2 * c * fc:2 * (c + 1) * fc])
        gate, up = gu[:, :fc], gu[:, fc:]
        act = (gate * jax.nn.sigmoid(gate) * up).astype(BF16)
        acc = acc + _dot(act, wout_ref[c * fc:(c + 1) * fc, :])
    if final:
        acc = _rms_norm(acc, fg_ref[...])
    out_ref[0] = acc


def _ffn_block(x, g, w_in, w_out, *, attn=None, final_g=None, tm=512, fc=256):
    bsz, seq, d = x.shape
    d_ff = w_out.shape[0]
    nc = d_ff // fc
    win = jnp.stack([w_in[:, :d_ff].reshape(d, nc, fc), w_in[:, d_ff:].reshape(d, nc, fc)], axis=2)
    win = win.reshape(d, 2 * d_ff).astype(BF16)
    tile = pl.BlockSpec((1, tm, d), lambda b, s: (b, s, 0))
    args, specs = [x], [tile]
    est = 2 * 2 * tm * d * 4 + (win.size + w_out.size) * 2 + 6 * tm * d * 4
    if attn is not None:
        o, w_o = attn
        args += [o, w_o.astype(BF16)]
        specs += [pl.BlockSpec((1, tm, o.shape[2]), lambda b, s: (b, s, 0)), _resident(w_o.shape)]
        est += 2 * tm * o.shape[2] * 2 + w_o.size * 2
    args += [g.reshape(1, d), win, w_out.astype(BF16)]
    specs += [_resident((1, d)), _resident(win.shape), _resident(w_out.shape)]
    if final_g is not None:
        args.append(final_g.reshape(1, d))
        specs.append(_resident((1, d)))
    return pl.pallas_call(
        functools.partial(_ffn_kernel, has_attn=attn is not None, final=final_g is not None, fc=fc),
        grid=(bsz, seq // tm),
        in_specs=specs,
        out_specs=tile,
        out_shape=jax.ShapeDtypeStruct((bsz, seq, d), F32),
        compiler_params=pltpu.CompilerParams(
            dimension_semantics=("parallel", "parallel"), vmem_limit_bytes=_vmem_limit(est)),
        name="ffn",
    )(*args)


def _key_of_row(p):
    return (p % V7X_SUBLANES) * KEY_VREGS + p // V7X_SUBLANES


def _qkv_kernel(x_ref, g_ref, wq_ref, wk_ref, wvt_ref, q_ref, k_ref, vt_ref, *, q_scale):
    ts = x_ref.shape[1]
    hn = _rms_norm(x_ref[0], g_ref[...]).astype(BF16)
    q_ref[0] = (_dot(hn, wq_ref[...]) * q_scale).astype(BF16)

    p = lax.broadcasted_iota(jnp.int32, (KEY_TILE, KEY_TILE), 0)
    col = lax.broadcasted_iota(jnp.int32, (KEY_TILE, KEY_TILE), 1)
    perm = (col == _key_of_row(p)).astype(BF16)
    hp = jnp.concatenate(
        [_dot(perm, hn[j * KEY_TILE:(j + 1) * KEY_TILE, :]) for j in range(ts // KEY_TILE)], axis=0).astype(BF16)

    k_ref[0] = _dot(hp, wk_ref[...]).astype(BF16)
    vt = _dot_nt(wvt_ref[...], hp).astype(BF16)
    for j in range(ts // KEY_TILE):
        vt_ref[0, j] = vt[:, j * KEY_TILE:(j + 1) * KEY_TILE]


def _qkv_proj(x, g, w_qkv, *, ts=512):
    bsz, seq, d = x.shape
    hd = w_qkv.shape[1] // 3
    wq = w_qkv[:, :hd].astype(BF16)
    wk = w_qkv[:, hd:2 * hd].astype(BF16)
    wvt = w_qkv[:, 2 * hd:].T.astype(BF16)
    q_scale = -math.log2(math.e) / math.sqrt(HEAD_DIM)
    est = 2 * ts * d * 4 + 3 * d * hd * 2 + 2 * 3 * ts * hd * 2 + 6 * ts * d * 4
    return pl.pallas_call(
        functools.partial(_qkv_kernel, q_scale=q_scale),
        grid=(bsz, seq // ts),
        in_specs=[
            pl.BlockSpec((1, ts, d), lambda b, s: (b, s, 0)),
            _resident((1, d)), _resident((d, hd)), _resident((d, hd)), _resident((hd, d)),
        ],
        out_specs=[
            pl.BlockSpec((1, ts, hd), lambda b, s: (b, s, 0)),
            pl.BlockSpec((1, ts, hd), lambda b, s: (b, s, 0)),
            pl.BlockSpec((1, ts // KEY_TILE, hd, KEY_TILE), lambda b, s: (b, s, 0, 0)),
        ],
        out_shape=[
            jax.ShapeDtypeStruct((bsz, seq, hd), BF16),
            jax.ShapeDtypeStruct((bsz, seq, hd), BF16),
            jax.ShapeDtypeStruct((bsz, seq // KEY_TILE, hd, KEY_TILE), BF16),
        ],
        compiler_params=pltpu.CompilerParams(
            dimension_semantics=("parallel", "parallel"), vmem_limit_bytes=_vmem_limit(est)),
        name="qkv",
    )(x, g.reshape(1, d), wq, wk, wvt)


def _shift_up(x, k):
    sub = lax.broadcasted_iota(jnp.int32, x.shape, 0)
    return jnp.where(sub < V7X_SUBLANES - k, pltpu.roll(x, V7X_SUBLANES - k, 0), 1.0)


def _attn_tile(kt, q_heads, k_ref, vt_ref, carry, t_of_lane):
    nq = q_heads[0].shape[0]
    k_blk = k_ref[0, pl.ds(pl.multiple_of(kt * KEY_TILE, KEY_TILE), KEY_TILE), :]
    if t_of_lane is not None:
        vreg = lax.broadcasted_iota(jnp.int32, (KEY_VREGS, V7X_SUBLANES, nq), 0)
        sub = lax.broadcasted_iota(jnp.int32, (KEY_VREGS, V7X_SUBLANES, nq), 1)
        valid = (kt * KEY_TILE + sub * KEY_VREGS + vreg) < t_of_lane
    out = []
    for h, (acc, c) in enumerate(carry):
        e = jnp.exp2(_dot_nt(k_blk, q_heads[h]))
        beta = 1.0 / (1.0 + e)
        f = 1.0 - beta
        beta = beta.reshape(KEY_VREGS, V7X_SUBLANES, nq)
        f = f.reshape(KEY_VREGS, V7X_SUBLANES, nq)
        if t_of_lane is not None:
            beta = jnp.where(valid, beta, 0.0)
            f = jnp.where(valid, f, 1.0)
        run = jnp.ones((V7X_SUBLANES, nq), F32)
        excl = [None] * KEY_VREGS
        for v in range(KEY_VREGS - 1, -1, -1):
            excl[v] = run
            run = run * f[v]
        inc = run
        for k in (1, 2, 4):
            inc = inc * _shift_up(inc, k)
        later = _shift_up(inc, 1) * c
        c = jnp.broadcast_to(inc[0:1, :], inc.shape) * c
        wgt = jnp.concatenate([beta[v] * (excl[v] * later) for v in range(KEY_VREGS)], axis=0).astype(BF16)
        v_blk = vt_ref[0, kt, h * HEAD_DIM:(h + 1) * HEAD_DIM, :]
        out.append((acc + _dot(v_blk, wgt), c))
    return tuple(out)


def _attn_kernel(q_ref, k_ref, vt_ref, o_ref):
    nq = q_ref.shape[1]
    qi = pl.program_id(2)
    q = q_ref[0]
    lane = lax.broadcasted_iota(jnp.int32, q.shape, 1)
    q_heads = [jnp.where(lane // HEAD_DIM == h, q, jnp.zeros_like(q)) for h in range(HEADS_PER_STEP)]
    carry = tuple((jnp.zeros((HEAD_DIM, nq), F32), jnp.ones((V7X_SUBLANES, nq), F32))
                  for _ in range(HEADS_PER_STEP))

    tiles_per_q = nq // KEY_TILE
    t_of_lane = qi * nq + lax.broadcasted_iota(jnp.int32, (KEY_VREGS, V7X_SUBLANES, nq), 2)
    for d in range(tiles_per_q - 1, -1, -1):
        carry = _attn_tile(qi * tiles_per_q + d, q_heads, k_ref, vt_ref, carry, t_of_lane)

    n_below = qi * tiles_per_q
    carry = lax.fori_loop(
        0, n_below, lambda j, cr: _attn_tile(n_below - 1 - j, q_heads, k_ref, vt_ref, cr, None), carry)

    o_t = jnp.concatenate([acc for acc, _ in carry], axis=0)
    o_ref[0] = o_t.T.astype(BF16)


def _attention(q, k, vt):
    bsz, seq, hd = q.shape
    n_kt = seq // KEY_TILE
    est = 2 * (2 * Q_TILE * V7X_LANES * 2 + seq * V7X_LANES * 2 * 2) + 64 * KEY_TILE * Q_TILE * 4
    return pl.pallas_call(
        _attn_kernel,
        grid=(bsz, hd // V7X_LANES, seq // Q_TILE),
        in_specs=[
            pl.BlockSpec((1, Q_TILE, V7X_LANES), lambda b, h, i: (b, i, h)),
            pl.BlockSpec((1, seq, V7X_LANES), lambda b, h, i: (b, 0, h)),
            pl.BlockSpec((1, n_kt, V7X_LANES, KEY_TILE), lambda b, h, i: (b, 0, h, 0)),
        ],
        out_specs=pl.BlockSpec((1, Q_TILE, V7X_LANES), lambda b, h, i: (b, i, h)),
        out_shape=jax.ShapeDtypeStruct((bsz, seq, hd), BF16),
        compiler_params=pltpu.CompilerParams(
            dimension_semantics=("parallel", "parallel", "arbitrary"), vmem_limit_bytes=_vmem_limit(est)),
        name="attn",
    )(q, k, vt)


def kernel(x, mix_norm, ffn_norm, final_norm, lru_w_in, lru_b_in, lru_conv_w, lru_conv_b, lru_gate_w, lru_gate_b,
           lru_lambda, lru_w_out, lru_b_out, attn_w_qkv, attn_w_o, ffn_w_in, ffn_w_out):
    depth = mix_norm.shape[0]
    assert x.shape[1] % 512 == 0 and x.shape[2] % V7X_LANES == 0
    h = x
    for layer in range(depth):
        j = layer // 2
        final_g = final_norm if layer == depth - 1 else None
        if layer % 2 == 0:
            h = _lru_block(h, mix_norm[layer], lru_w_in[j], lru_b_in[j], lru_conv_w[j], lru_conv_b[j],
                           lru_gate_w[j], lru_gate_b[j], lru_lambda[j], lru_w_out[j], lru_b_out[j])
            attn = None
        else:
            q, k, vt = _qkv_proj(h, mix_norm[layer], attn_w_qkv[j])
            attn = (_attention(q, k, vt), attn_w_o[j])
        h = _ffn_block(h, ffn_norm[layer], ffn_w_in[layer], ffn_w_out[layer], attn=attn, final_g=final_g)
    return h
```

```python
import functools
import math

import jax
import jax.numpy as jnp
from jax import lax
from jax.experimental import pallas as pl
from jax.experimental.pallas import tpu as pltpu

RMS_EPS = 1e-6
LRU_C = 8.0
HEAD_DIM = 64
N_LRU_BLOCKS = 4

V7X_LANES = 128
V7X_SUBLANES = 8
V7X_VMEM_BYTES = 64 * 1024 * 1024

KEY_TILE = 128
KEY_VREGS = KEY_TILE // V7X_SUBLANES
Q_TILE = 256
HEADS_PER_STEP = V7X_LANES // HEAD_DIM
BAND_BELOW = 2
TAIL_TILES = 2

BF16 = jnp.bfloat16
F32 = jnp.float32


def _vmem_limit(n_bytes):
    return int(min(n_bytes * 3 // 2 + (4 << 20), V7X_VMEM_BYTES - (4 << 20)))


def _rms_norm(x, g):
    return x * lax.rsqrt(jnp.mean(x * x, axis=-1, keepdims=True) + RMS_EPS) * g


def _dot(a, b):
    return jnp.dot(a, b, preferred_element_type=F32)


def _dot_nt(a, b):
    return lax.dot_general(a, b, (((1,), (1,)), ((), ())), preferred_element_type=F32)


def _resident(shape):
    zeros = (0,) * len(shape)
    return pl.BlockSpec(shape, lambda *_: zeros, pipeline_mode=pl.Buffered(1))


def _tile_scan(a, b):
    rows, w = a.shape
    a3 = a.reshape(rows // V7X_SUBLANES, V7X_SUBLANES, w)
    b3 = b.reshape(rows // V7X_SUBLANES, V7X_SUBLANES, w)
    sub = lax.broadcasted_iota(jnp.int32, (1, V7X_SUBLANES, w), 1)
    for k in (1, 2, 4):
        keep = sub >= k
        a_prev = jnp.where(keep, pltpu.roll(a3, k, 1), 1.0)
        b_prev = jnp.where(keep, pltpu.roll(b3, k, 1), 0.0)
        b3 = a3 * b_prev + b3
        a3 = a3 * a_prev
    return a3.reshape(rows, w), b3.reshape(rows, w)


def _lru_kernel(x_ref, g_ref, win_ref, bin_ref, cw_ref, cb_ref, gw_ref, gb_ref, lam_ref, wout_ref, bout_ref,
                o_ref, rec_buf, h_carry, *, conv_width):
    ts = x_ref.shape[1]
    w = wout_ref.shape[0]
    blk = w // N_LRU_BLOCKS
    pad = V7X_SUBLANES

    @pl.when(pl.program_id(1) == 0)
    def _():
        rec_buf[0:pad, :] = jnp.zeros((pad, w), F32)
        h_carry[...] = jnp.zeros_like(h_carry)

    x = x_ref[0]
    hn = _rms_norm(x, g_ref[...]).astype(BF16)
    u = _dot(hn, win_ref[...]) + bin_ref[...]
    gate_branch = u[:, :w]
    rec_buf[pad:pad + ts, :] = u[:, w:]

    conv = cb_ref[...]
    for k in range(conv_width):
        off = pad - (conv_width - 1) + k
        conv = conv + cw_ref[k:k + 1, :] * rec_buf[off:off + ts, :]
    rec_buf[0:pad, :] = rec_buf[ts:ts + pad, :]

    conv16 = conv.astype(BF16)
    gl = [_dot(conv16[:, n * blk:(n + 1) * blk], gw_ref[n]) for n in range(N_LRU_BLOCKS)]
    gl_r = jnp.concatenate([t[:, :blk] for t in gl], axis=1) + gb_ref[0:1, :]
    gl_i = jnp.concatenate([t[:, blk:] for t in gl], axis=1) + gb_ref[1:2, :]
    r_gate = jax.nn.sigmoid(gl_r)
    i_gate = jax.nn.sigmoid(gl_i)

    nlam = -lam_ref[...]
    softplus_nlam = jnp.maximum(nlam, 0.0) + jnp.log1p(jnp.exp(-jnp.abs(nlam)))
    log_a = (-LRU_C * softplus_nlam) * r_gate
    a = jnp.exp(log_a)
    mult = jnp.sqrt(1.0 - a * a)
    b = mult * (i_gate * conv)

    a_cum, b_cum = _tile_scan(a, b)
    hc = h_carry[...]
    hs = []
    for j in range(ts // V7X_SUBLANES):
        sl = slice(j * V7X_SUBLANES, (j + 1) * V7X_SUBLANES)
        h8 = a_cum[sl] * hc + b_cum[sl]
        hs.append(h8)
        hc = jnp.broadcast_to(h8[V7X_SUBLANES - 1:, :], h8.shape)
    h_carry[...] = hc
    h = jnp.concatenate(hs, axis=0)

    y = (jax.nn.gelu(gate_branch, approximate=True) * h).astype(BF16)
    o_ref[0] = _dot(y, wout_ref[...]) + bout_ref[...] + x


def _lru_block(x, g, w_in, b_in, conv_w, conv_b, gate_w, gate_b, lam, w_out, b_out, *, ts=256):
    bsz, seq, d = x.shape
    w = w_out.shape[0]
    conv_width = conv_w.shape[0]
    blk = w // N_LRU_BLOCKS
    gw = jnp.concatenate([gate_w[0], gate_w[1]], axis=-1).astype(BF16)
    row = lambda v: v.reshape(1, -1)
    est = (2 * 2 * ts * d * 4 + w_in.size * 2 + gw.size * 2 + w_out.size * 2
           + 8 * ts * w * 4)
    return pl.pallas_call(
        functools.partial(_lru_kernel, conv_width=conv_width),
        grid=(bsz, seq // ts),
        in_specs=[
            pl.BlockSpec((1, ts, d), lambda b, s: (b, s, 0)),
            _resident((1, d)),
            _resident((d, 2 * w)),
            _resident((1, 2 * w)),
            _resident((conv_width, w)),
            _resident((1, w)),
            _resident((N_LRU_BLOCKS, blk, 2 * blk)),
            _resident((2, w)),
            _resident((1, w)),
            _resident((w, d)),
            _resident((1, d)),
        ],
        out_specs=pl.BlockSpec((1, ts, d), lambda b, s: (b, s, 0)),
        out_shape=jax.ShapeDtypeStruct((bsz, seq, d), F32),
        scratch_shapes=[pltpu.VMEM((ts + 2 * V7X_SUBLANES, w), F32), pltpu.VMEM((V7X_SUBLANES, w), F32)],
        compiler_params=pltpu.CompilerParams(
            dimension_semantics=("parallel", "arbitrary"), vmem_limit_bytes=_vmem_limit(est)),
        name="lru",
    )(x, row(g), w_in.astype(BF16), row(b_in), conv_w, row(conv_b), gw, gate_b, row(lam),
      w_out.astype(BF16), row(b_out))


def _ffn_kernel(*refs, has_attn, final, fc):
    refs = list(refs)
    x_ref = refs.pop(0)
    o_ref, wo_ref = (refs.pop(0), refs.pop(0)) if has_attn else (None, None)
    g_ref, win_ref, wout_ref = refs.pop(0), refs.pop(0), refs.pop(0)
    fg_ref = refs.pop(0) if final else None
    out_ref = refs.pop(0)

    h = x_ref[0]
    if has_attn:
        h = h + _dot(o_ref[0], wo_ref[...])
    hn = _rms_norm(h, g_ref[...]).astype(BF16)
    acc = h
    for c in range(wout_ref.shape[0] // fc):
        gu = _dot(hn, win_ref[:, 2 * c * fc:2 * (c + 1) * fc])
        gate, up = gu[:, :fc], gu[:, fc:]
        act = (gate * jax.nn.sigmoid(gate) * up).astype(BF16)
        acc = acc + _dot(act, wout_ref[c * fc:(c + 1) * fc, :])
    if final:
        acc = _rms_norm(acc, fg_ref[...])
    out_ref[0] = acc


def _ffn_block(x, g, w_in, w_out, *, attn=None, final_g=None, tm=512, fc=256):
    bsz, seq, d = x.shape
    d_ff = w_out.shape[0]
    nc = d_ff // fc
    win = jnp.stack([w_in[:, :d_ff].reshape(d, nc, fc), w_in[:, d_ff:].reshape(d, nc, fc)], axis=2)
    win = win.reshape(d, 2 * d_ff).astype(BF16)
    tile = pl.BlockSpec((1, tm, d), lambda b, s: (b, s, 0))
    args, specs = [x], [tile]
    est = 2 * 2 * tm * d * 4 + (win.size + w_out.size) * 2 + 6 * tm * d * 4
    if attn is not None:
        o, w_o = attn
        args += [o, w_o.astype(BF16)]
        specs += [pl.BlockSpec((1, tm, o.shape[2]), lambda b, s: (b, s, 0)), _resident(w_o.shape)]
        est += 2 * tm * o.shape[2] * 2 + w_o.size * 2
    args += [g.reshape(1, d), win, w_out.astype(BF16)]
    specs += [_resident((1, d)), _resident(win.shape), _resident(w_out.shape)]
    if final_g is not None:
        args.append(final_g.reshape(1, d))
        specs.append(_resident((1, d)))
    return pl.pallas_call(
        functools.partial(_ffn_kernel, has_attn=attn is not None, final=final_g is not None, fc=fc),
        grid=(bsz, seq // tm),
        in_specs=specs,
        out_specs=tile,
        out_shape=jax.ShapeDtypeStruct((bsz, seq, d), F32),
        compiler_params=pltpu.CompilerParams(
            dimension_semantics=("parallel", "parallel"), vmem_limit_bytes=_vmem_limit(est)),
        name="ffn",
    )(*args)


def _key_of_row(p):
    return (p % V7X_SUBLANES) * KEY_VREGS + p // V7X_SUBLANES


def _qkv_kernel(x_ref, g_ref, wq_ref, wk_ref, wvt_ref, q_ref, k_ref, vt_ref, *, q_scale):
    ts = x_ref.shape[1]
    hn = _rms_norm(x_ref[0], g_ref[...]).astype(BF16)
    q_ref[0] = (_dot(hn, wq_ref[...]) * q_scale).astype(BF16)

    p = lax.broadcasted_iota(jnp.int32, (KEY_TILE, KEY_TILE), 0)
    col = lax.broadcasted_iota(jnp.int32, (KEY_TILE, KEY_TILE), 1)
    perm = (col == _key_of_row(p)).astype(BF16)
    hp = jnp.concatenate(
        [_dot(perm, hn[j * KEY_TILE:(j + 1) * KEY_TILE, :]) for j in range(ts // KEY_TILE)], axis=0).astype(BF16)

    k_ref[0] = _dot(hp, wk_ref[...]).astype(BF16)
    vt = _dot_nt(wvt_ref[...], hp).astype(BF16)
    for j in range(ts // KEY_TILE):
        vt_ref[0, j] = vt[:, j * KEY_TILE:(j + 1) * KEY_TILE]


def _qkv_proj(x, g, w_qkv, *, ts=512):
    bsz, seq, d = x.shape
    hd = w_qkv.shape[1] // 3
    wq = w_qkv[:, :hd].astype(BF16)
    wk = w_qkv[:, hd:2 * hd].astype(BF16)
    wvt = w_qkv[:, 2 * hd:].T.astype(BF16)
    q_scale = -math.log2(math.e) / math.sqrt(HEAD_DIM)
    est = 2 * ts * d * 4 + 3 * d * hd * 2 + 2 * 3 * ts * hd * 2 + 6 * ts * d * 4
    return pl.pallas_call(
        functools.partial(_qkv_kernel, q_scale=q_scale),
        grid=(bsz, seq // ts),
        in_specs=[
            pl.BlockSpec((1, ts, d), lambda b, s: (b, s, 0)),
            _resident((1, d)), _resident((d, hd)), _resident((d, hd)), _resident((hd, d)),
        ],
        out_specs=[
            pl.BlockSpec((1, ts, hd), lambda b, s: (b, s, 0)),
            pl.BlockSpec((1, ts, hd), lambda b, s: (b, s, 0)),
            pl.BlockSpec((1, ts // KEY_TILE, hd, KEY_TILE), lambda b, s: (b, s, 0, 0)),
        ],
        out_shape=[
            jax.ShapeDtypeStruct((bsz, seq, hd), BF16),
            jax.ShapeDtypeStruct((bsz, seq, hd), BF16),
            jax.ShapeDtypeStruct((bsz, seq // KEY_TILE, hd, KEY_TILE), BF16),
        ],
        compiler_params=pltpu.CompilerParams(
            dimension_semantics=("parallel", "parallel"), vmem_limit_bytes=_vmem_limit(est)),
        name="qkv",
    )(x, g.reshape(1, d), wq, wk, wvt)


def _shift_up(x, k):
    sub = lax.broadcasted_iota(jnp.int32, x.shape, 0)
    return jnp.where(sub < V7X_SUBLANES - k, pltpu.roll(x, V7X_SUBLANES - k, 0), 1.0)


def _tile_weights(s_t, c, valid):
    nq = s_t.shape[1]
    beta = 1.0 / (1.0 + jnp.exp2(s_t))
    f = 1.0 - beta
    beta = beta.reshape(KEY_VREGS, V7X_SUBLANES, nq)
    f = f.reshape(KEY_VREGS, V7X_SUBLANES, nq)
    if valid is not None:
        beta = jnp.where(valid, beta, 0.0)
        f = jnp.where(valid, f, 1.0)
    run = jnp.ones((V7X_SUBLANES, nq), F32)
    excl = [None] * KEY_VREGS
    for v in range(KEY_VREGS - 1, -1, -1):
        excl[v] = run
        run = run * f[v]
    inc = run
    for k in (1, 2, 4):
        inc = inc * _shift_up(inc, k)
    later = _shift_up(inc, 1) * c
    c = jnp.broadcast_to(inc[0:1, :], inc.shape) * c
    wgt = jnp.concatenate([beta[v] * (excl[v] * later) for v in range(KEY_VREGS)], axis=0)
    return wgt.astype(BF16), c


def _attn_band(kt0, n_tiles, n_diag, *, t_of_lane, q_heads, k_ref, vt_ref, carry):
    nq = q_heads[0].shape[0]
    k_rows = k_ref[0, pl.ds(pl.multiple_of(kt0 * KEY_TILE, KEY_TILE), n_tiles * KEY_TILE), :]
    vreg = lax.broadcasted_iota(jnp.int32, (KEY_VREGS, V7X_SUBLANES, nq), 0)
    sub = lax.broadcasted_iota(jnp.int32, (KEY_VREGS, V7X_SUBLANES, nq), 1)
    out = []
    for h, (acc, c) in enumerate(carry):
        s_t = _dot_nt(k_rows, q_heads[h])
        for d in range(n_tiles - 1, -1, -1):
            valid = None
            if d >= n_tiles - n_diag:
                valid = ((kt0 + d) * KEY_TILE + sub * KEY_VREGS + vreg) < t_of_lane
            wgt, c = _tile_weights(s_t[d * KEY_TILE:(d + 1) * KEY_TILE, :], c, valid)
            acc = acc + _dot(vt_ref[0, kt0 + d, h * HEAD_DIM:(h + 1) * HEAD_DIM, :], wgt)
        out.append((acc, c))
    return tuple(out)


def _attn_kernel(q_ref, k_ref, vt_ref, o_ref):
    nq = q_ref.shape[1]
    qi = pl.program_id(2)
    q = q_ref[0]
    lane = lax.broadcasted_iota(jnp.int32, q.shape, 1)
    q_heads = [jnp.where(lane // HEAD_DIM == h, q, jnp.zeros_like(q)) for h in range(HEADS_PER_STEP)]
    carry = tuple((jnp.zeros((HEAD_DIM, nq), F32), jnp.ones((V7X_SUBLANES, nq), F32))
                  for _ in range(HEADS_PER_STEP))
    t_of_lane = qi * nq + lax.broadcasted_iota(jnp.int32, (KEY_VREGS, V7X_SUBLANES, nq), 2)
    band = functools.partial(_attn_band, t_of_lane=t_of_lane, q_heads=q_heads, k_ref=k_ref, vt_ref=vt_ref)
    n_diag = nq // KEY_TILE
    diag0 = qi * n_diag

    carry = lax.cond(
        qi == 0,
        lambda: band(0, n_diag, n_diag, carry=carry),
        lambda: band(diag0 - BAND_BELOW, BAND_BELOW + n_diag, n_diag, carry=carry))

    def alive(cr):
        return (functools.reduce(jnp.maximum, [jnp.max(c) for _, c in cr]) > 0.0).astype(jnp.int32)

    def more(state):
        kt_hi, live, _ = state
        return jnp.logical_and(kt_hi >= TAIL_TILES - 1, live > 0)

    def step(state):
        kt_hi, _, cr = state
        cr = band(kt_hi - (TAIL_TILES - 1), TAIL_TILES, 0, carry=cr)
        return kt_hi - TAIL_TILES, alive(cr), cr

    _, _, carry = lax.while_loop(more, step, (diag0 - BAND_BELOW - 1, alive(carry), carry))

    o_t = jnp.concatenate([acc for acc, _ in carry], axis=0)
    o_ref[0] = o_t.T.astype(BF16)


def _attention(q, k, vt):
    bsz, seq, hd = q.shape
    n_kt = seq // KEY_TILE
    assert BAND_BELOW % TAIL_TILES == 0 and (Q_TILE // KEY_TILE) % TAIL_TILES == 0
    est = 2 * (2 * Q_TILE * V7X_LANES * 2 + seq * V7X_LANES * 2 * 2) + 64 * KEY_TILE * Q_TILE * 4
    return pl.pallas_call(
        _attn_kernel,
        grid=(bsz, hd // V7X_LANES, seq // Q_TILE),
        in_specs=[
            pl.BlockSpec((1, Q_TILE, V7X_LANES), lambda b, h, i: (b, i, h)),
            pl.BlockSpec((1, seq, V7X_LANES), lambda b, h, i: (b, 0, h)),
            pl.BlockSpec((1, n_kt, V7X_LANES, KEY_TILE), lambda b, h, i: (b, 0, h, 0)),
        ],
        out_specs=pl.BlockSpec((1, Q_TILE, V7X_LANES), lambda b, h, i: (b, i, h)),
        out_shape=jax.ShapeDtypeStruct((bsz, seq, hd), BF16),
        compiler_params=pltpu.CompilerParams(
            dimension_semantics=("parallel", "parallel", "arbitrary"), vmem_limit_bytes=_vmem_limit(est)),
        name="attn",
    )(q, k, vt)


def kernel(x, mix_norm, ffn_norm, final_norm, lru_w_in, lru_b_in, lru_conv_w, lru_conv_b, lru_gate_w, lru_gate_b,
           lru_lambda, lru_w_out, lru_b_out, attn_w_qkv, attn_w_o, ffn_w_in, ffn_w_out):
    depth = mix_norm.shape[0]
    assert x.shape[1] % 512 == 0 and x.shape[2] % V7X_LANES == 0
    h = x
    for layer in range(depth):
        j = layer // 2
        final_g = final_norm if layer == depth - 1 else None
        if layer % 2 == 0:
            h = _lru_block(h, mix_norm[layer], lru_w_in[j], lru_b_in[j], lru_conv_w[j], lru_conv_b[j],
                           lru_gate_w[j], lru_gate_b[j], lru_lambda[j], lru_w_out[j], lru_b_out[j])
            attn = None
        else:
            q, k, vt = _qkv_proj(h, mix_norm[layer], attn_w_qkv[j])
            attn = (_attention(q, k, vt), attn_w_o[j])
        h = _ffn_block(h, ffn_norm[layer], ffn_w_in[layer], ffn_w_out[layer], attn=attn, final_g=final_g)
    return h
```

```python
import functools
import math

import jax
import jax.numpy as jnp
from jax import lax
from jax.experimental import pallas as pl
from jax.experimental.pallas import tpu as pltpu

RMS_EPS = 1e-6
LRU_C = 8.0
HEAD_DIM = 64
N_LRU_BLOCKS = 4

V7X_LANES = 128
V7X_SUBLANES = 8
V7X_VMEM_BYTES = 64 * 1024 * 1024

KEY_TILE = 128
KEY_VREGS = KEY_TILE // V7X_SUBLANES
Q_TILE = 256
HEADS_PER_BLOCK = V7X_LANES // HEAD_DIM
ATTN_LANE_BLOCKS = 4
SCORE_TILES = 2
BAND_BELOW = 2
TAIL_TILES = 2

BF16 = jnp.bfloat16
F32 = jnp.float32


def _vmem_limit(n_bytes):
    return int(min(n_bytes * 3 // 2 + (4 << 20), V7X_VMEM_BYTES - (4 << 20)))


def _rms_norm(x, g):
    return x * lax.rsqrt(jnp.mean(x * x, axis=-1, keepdims=True) + RMS_EPS) * g


def _dot(a, b):
    return jnp.dot(a, b, preferred_element_type=F32)


def _dot_nt(a, b):
    return lax.dot_general(a, b, (((1,), (1,)), ((), ())), preferred_element_type=F32)


def _resident(shape):
    zeros = (0,) * len(shape)
    return pl.BlockSpec(shape, lambda *_: zeros, pipeline_mode=pl.Buffered(1))


def _tile_scan(a, b):
    rows, w = a.shape
    a3 = a.reshape(rows // V7X_SUBLANES, V7X_SUBLANES, w)
    b3 = b.reshape(rows // V7X_SUBLANES, V7X_SUBLANES, w)
    sub = lax.broadcasted_iota(jnp.int32, (1, V7X_SUBLANES, w), 1)
    for k in (1, 2, 4):
        keep = sub >= k
        a_prev = jnp.where(keep, pltpu.roll(a3, k, 1), 1.0)
        b_prev = jnp.where(keep, pltpu.roll(b3, k, 1), 0.0)
        b3 = a3 * b_prev + b3
        a3 = a3 * a_prev
    return a3.reshape(rows, w), b3.reshape(rows, w)


def _lru_kernel(x_ref, g_ref, win_ref, bin_ref, cw_ref, cb_ref, gw_ref, gb_ref, lam_ref, wout_ref, bout_ref,
                o_ref, rec_buf, h_carry, *, conv_width):
    ts = x_ref.shape[1]
    w = wout_ref.shape[0]
    blk = w // N_LRU_BLOCKS
    pad = V7X_SUBLANES

    @pl.when(pl.program_id(1) == 0)
    def _():
        rec_buf[0:pad, :] = jnp.zeros((pad, w), F32)
        h_carry[...] = jnp.zeros_like(h_carry)

    x = x_ref[0]
    hn = _rms_norm(x, g_ref[...]).astype(BF16)
    u = _dot(hn, win_ref[...]) + bin_ref[...]
    gate_branch = u[:, :w]
    rec_buf[pad:pad + ts, :] = u[:, w:]

    conv = cb_ref[...]
    for k in range(conv_width):
        off = pad - (conv_width - 1) + k
        conv = conv + cw_ref[k:k + 1, :] * rec_buf[off:off + ts, :]
    rec_buf[0:pad, :] = rec_buf[ts:ts + pad, :]

    conv16 = conv.astype(BF16)
    gl = [_dot(conv16[:, n * blk:(n + 1) * blk], gw_ref[n]) for n in range(N_LRU_BLOCKS)]
    gl_r = jnp.concatenate([t[:, :blk] for t in gl], axis=1) + gb_ref[0:1, :]
    gl_i = jnp.concatenate([t[:, blk:] for t in gl], axis=1) + gb_ref[1:2, :]
    r_gate = jax.nn.sigmoid(gl_r)
    i_gate = jax.nn.sigmoid(gl_i)

    nlam = -lam_ref[...]
    softplus_nlam = jnp.maximum(nlam, 0.0) + jnp.log1p(jnp.exp(-jnp.abs(nlam)))
    log_a = (-LRU_C * softplus_nlam) * r_gate
    a = jnp.exp(log_a)
    mult = jnp.sqrt(1.0 - a * a)
    b = mult * (i_gate * conv)

    a_cum, b_cum = _tile_scan(a, b)
    hc = h_carry[...]
    hs = []
    for j in range(ts // V7X_SUBLANES):
        sl = slice(j * V7X_SUBLANES, (j + 1) * V7X_SUBLANES)
        h8 = a_cum[sl] * hc + b_cum[sl]
        hs.append(h8)
        hc = jnp.broadcast_to(h8[V7X_SUBLANES - 1:, :], h8.shape)
    h_carry[...] = hc
    h = jnp.concatenate(hs, axis=0)

    y = (jax.nn.gelu(gate_branch, approximate=True) * h).astype(BF16)
    o_ref[0] = _dot(y, wout_ref[...]) + bout_ref[...] + x


def _lru_block(x, g, w_in, b_in, conv_w, conv_b, gate_w, gate_b, lam, w_out, b_out, *, ts=256):
    bsz, seq, d = x.shape
    w = w_out.shape[0]
    conv_width = conv_w.shape[0]
    blk = w // N_LRU_BLOCKS
    gw = jnp.concatenate([gate_w[0], gate_w[1]], axis=-1).astype(BF16)
    row = lambda v: v.reshape(1, -1)
    est = (2 * 2 * ts * d * 4 + w_in.size * 2 + gw.size * 2 + w_out.size * 2
           + 8 * ts * w * 4)
    return pl.pallas_call(
        functools.partial(_lru_kernel, conv_width=conv_width),
        grid=(bsz, seq // ts),
        in_specs=[
            pl.BlockSpec((1, ts, d), lambda b, s: (b, s, 0)),
            _resident((1, d)),
            _resident((d, 2 * w)),
            _resident((1, 2 * w)),
            _resident((conv_width, w)),
            _resident((1, w)),
            _resident((N_LRU_BLOCKS, blk, 2 * blk)),
            _resident((2, w)),
            _resident((1, w)),
            _resident((w, d)),
            _resident((1, d)),
        ],
        out_specs=pl.BlockSpec((1, ts, d), lambda b, s: (b, s, 0)),
        out_shape=jax.ShapeDtypeStruct((bsz, seq, d), F32),
        scratch_shapes=[pltpu.VMEM((ts + 2 * V7X_SUBLANES, w), F32), pltpu.VMEM((V7X_SUBLANES, w), F32)],
        compiler_params=pltpu.CompilerParams(
            dimension_semantics=("parallel", "arbitrary"), vmem_limit_bytes=_vmem_limit(est)),
        name="lru",
    )(x, row(g), w_in.astype(BF16), row(b_in), conv_w, row(conv_b), gw, gate_b, row(lam),
      w_out.astype(BF16), row(b_out))


def _ffn_kernel(*refs, has_attn, final, fc):
    refs = list(refs)
    x_ref = refs.pop(0)
    o_ref, wo_ref = (refs.pop(0), refs.pop(0)) if has_attn else (None, None)
    g_ref, win_ref, wout_ref = refs.pop(0), refs.pop(0), refs.pop(0)
    fg_ref = refs.pop(0) if final else None
    out_ref = refs.pop(0)

    h = x_ref[0]
    if has_attn:
        h = h + _dot(o_ref[0], wo_ref[...])
    hn = _rms_norm(h, g_ref[...]).astype(BF16)
    acc = h
    d_ff = wout_ref.shape[0]

    def gate_up(c):
        return (_dot(hn, win_ref[:, c * fc:(c + 1) * fc]),
                _dot(hn, win_ref[:, d_ff + c * fc:d_ff + (c + 1) * fc]))

    nxt = gate_up(0)
    for c in range(d_ff // fc):
        gate, up = nxt
        if (c + 1) * fc < d_ff:
            nxt = gate_up(c + 1)
        act = (gate * jax.nn.sigmoid(gate) * up).astype(BF16)
        acc = acc + _dot(act, wout_ref[c * fc:(c + 1) * fc, :])
    if final:
        acc = _rms_norm(acc, fg_ref[...])
    out_ref[0] = acc


def _ffn_block(x, g, w_in, w_out, *, attn=None, final_g=None, tm=512, fc=256):
    bsz, seq, d = x.shape
    d_ff = w_out.shape[0]
    assert d_ff % fc == 0 and fc % V7X_LANES == 0
    win = w_in.astype(BF16)
    tile = pl.BlockSpec((1, tm, d), lambda b, s: (b, s, 0))
    args, specs = [x], [tile]
    est = 2 * 2 * tm * d * 4 + (win.size + w_out.size) * 2 + 6 * tm * d * 4
    if attn is not None:
        o, w_o = attn
        args += [o, w_o.astype(BF16)]
        specs += [pl.BlockSpec((1, tm, o.shape[2]), lambda b, s: (b, s, 0)), _resident(w_o.shape)]
        est += 2 * tm * o.shape[2] * 2 + w_o.size * 2
    args += [g.reshape(1, d), win, w_out.astype(BF16)]
    specs += [_resident((1, d)), _resident(win.shape), _resident(w_out.shape)]
    if final_g is not None:
        args.append(final_g.reshape(1, d))
        specs.append(_resident((1, d)))
    return pl.pallas_call(
        functools.partial(_ffn_kernel, has_attn=attn is not None, final=final_g is not None, fc=fc),
        grid=(bsz, seq // tm),
        in_specs=specs,
        out_specs=tile,
        out_shape=jax.ShapeDtypeStruct((bsz, seq, d), F32),
        compiler_params=pltpu.CompilerParams(
            dimension_semantics=("parallel", "parallel"), vmem_limit_bytes=_vmem_limit(est)),
        name="ffn",
    )(*args)


def _key_of_row(p):
    return (p % V7X_SUBLANES) * KEY_VREGS + p // V7X_SUBLANES


def _qkv_kernel(x_ref, g_ref, wq_ref, wk_ref, wvt_ref, q_ref, k_ref, vt_ref, *, q_scale):
    ts = x_ref.shape[1]
    hn = _rms_norm(x_ref[0], g_ref[...]).astype(BF16)
    q_ref[0] = (_dot(hn, wq_ref[...]) * q_scale).astype(BF16)

    p = lax.broadcasted_iota(jnp.int32, (KEY_TILE, KEY_TILE), 0)
    col = lax.broadcasted_iota(jnp.int32, (KEY_TILE, KEY_TILE), 1)
    perm = (col == _key_of_row(p)).astype(BF16)
    hp = jnp.concatenate(
        [_dot(perm, hn[j * KEY_TILE:(j + 1) * KEY_TILE, :]) for j in range(ts // KEY_TILE)], axis=0).astype(BF16)

    k_ref[0] = _dot(hp, wk_ref[...]).astype(BF16)
    vt = _dot_nt(wvt_ref[...], hp).astype(BF16)
    for j in range(ts // KEY_TILE):
        vt_ref[0, j] = vt[:, j * KEY_TILE:(j + 1) * KEY_TILE]


def _qkv_proj(x, g, w_qkv, *, ts=512):
    bsz, seq, d = x.shape
    hd = w_qkv.shape[1] // 3
    wq = w_qkv[:, :hd].astype(BF16)
    wk = w_qkv[:, hd:2 * hd].astype(BF16)
    wvt = w_qkv[:, 2 * hd:].T.astype(BF16)
    q_scale = -math.log2(math.e) / math.sqrt(HEAD_DIM)
    est = 2 * ts * d * 4 + 3 * d * hd * 2 + 2 * 3 * ts * hd * 2 + 6 * ts * d * 4
    return pl.pallas_call(
        functools.partial(_qkv_kernel, q_scale=q_scale),
        grid=(bsz, seq // ts),
        in_specs=[
            pl.BlockSpec((1, ts, d), lambda b, s: (b, s, 0)),
            _resident((1, d)), _resident((d, hd)), _resident((d, hd)), _resident((hd, d)),
        ],
        out_specs=[
            pl.BlockSpec((1, ts, hd), lambda b, s: (b, s, 0)),
            pl.BlockSpec((1, ts, hd), lambda b, s: (b, s, 0)),
            pl.BlockSpec((1, ts // KEY_TILE, hd, KEY_TILE), lambda b, s: (b, s, 0, 0)),
        ],
        out_shape=[
            jax.ShapeDtypeStruct((bsz, seq, hd), BF16),
            jax.ShapeDtypeStruct((bsz, seq, hd), BF16),
            jax.ShapeDtypeStruct((bsz, seq // KEY_TILE, hd, KEY_TILE), BF16),
        ],
        compiler_params=pltpu.CompilerParams(
            dimension_semantics=("parallel", "parallel"), vmem_limit_bytes=_vmem_limit(est)),
        name="qkv",
    )(x, g.reshape(1, d), wq, wk, wvt)


def _shift_up(x, k):
    sub = lax.broadcasted_iota(jnp.int32, x.shape, 0)
    return jnp.where(sub < V7X_SUBLANES - k, pltpu.roll(x, V7X_SUBLANES - k, 0), 1.0)


def _group_weights(s_g, c, valid):
    shape3 = (KEY_VREGS, V7X_SUBLANES, V7X_LANES)
    beta = 1.0 / (1.0 + jnp.exp2(s_g))
    f = 1.0 - beta
    beta = beta.reshape(shape3)
    f = f.reshape(shape3)
    if valid is not None:
        beta = jnp.where(valid, beta, 0.0)
        f = jnp.where(valid, f, 1.0)
    run = jnp.ones(shape3[1:], F32)
    excl = [None] * KEY_VREGS
    for v in range(KEY_VREGS - 1, -1, -1):
        excl[v] = run
        run = run * f[v]
    inc = run
    for k in (1, 2, 4):
        inc = inc * _shift_up(inc, k)
    later = _shift_up(inc, 1) * c
    c = jnp.broadcast_to(inc[0:1, :], inc.shape) * c
    wgt = jnp.concatenate([beta[v] * (excl[v] * later) for v in range(KEY_VREGS)], axis=0)
    return wgt.astype(BF16), c


def _attn_band(kt0, n_tiles, n_diag, *, tri, q_cat, k_ref, vt_ref, carry):
    n_groups = len(carry[0][1])
    nq = n_groups * V7X_LANES
    scores = {}
    for p in range(n_tiles // SCORE_TILES - 1, -1, -1):
        row0 = pl.multiple_of((kt0 + p * SCORE_TILES) * KEY_TILE, KEY_TILE)
        rows = k_ref[0, pl.ds(row0, SCORE_TILES * KEY_TILE), :]
        for lb, q_lb in enumerate(q_cat):
            scores[p, lb] = _dot_nt(rows[:, lb * V7X_LANES:(lb + 1) * V7X_LANES], q_lb)

    carry = [(acc, list(cs)) for acc, cs in carry]
    for d in range(n_tiles - 1, -1, -1):
        p, r = divmod(d, SCORE_TILES)
        j = d - (n_tiles - n_diag)
        for hi, (acc, cs) in enumerate(carry):
            lb, hh = divmod(hi, HEADS_PER_BLOCK)
            w_groups = []
            for g in range(n_groups):
                if g < j:
                    w_groups.append(jnp.zeros((KEY_TILE, V7X_LANES), BF16))
                    continue
                col0 = hh * nq + g * V7X_LANES
                s_g = scores[p, lb][r * KEY_TILE:(r + 1) * KEY_TILE, col0:col0 + V7X_LANES]
                w_g, cs[g] = _group_weights(s_g, cs[g], tri if g == j else None)
                w_groups.append(w_g)
            v_blk = vt_ref[0, kt0 + d, hi * HEAD_DIM:(hi + 1) * HEAD_DIM, :]
            carry[hi] = (acc + _dot(v_blk, jnp.concatenate(w_groups, axis=1)), cs)
    return tuple((acc, tuple(cs)) for acc, cs in carry)


def _attn_kernel(q_ref, k_ref, vt_ref, o_ref):
    nq = q_ref.shape[1]
    n_blocks = q_ref.shape[2] // V7X_LANES
    qi = pl.program_id(2)
    lane = lax.broadcasted_iota(jnp.int32, (nq, V7X_LANES), 1)
    q_cat = []
    for lb in range(n_blocks):
        q = q_ref[0, :, lb * V7X_LANES:(lb + 1) * V7X_LANES]
        q_cat.append(jnp.concatenate(
            [jnp.where(lane // HEAD_DIM == hh, q, jnp.zeros_like(q)) for hh in range(HEADS_PER_BLOCK)], axis=0))
    carry = tuple((jnp.zeros((HEAD_DIM, nq), F32),
                   tuple(jnp.ones((V7X_SUBLANES, V7X_LANES), F32) for _ in range(nq // V7X_LANES)))
                  for _ in range(n_blocks * HEADS_PER_BLOCK))
    shape3 = (KEY_VREGS, V7X_SUBLANES, V7X_LANES)
    tri = (lax.broadcasted_iota(jnp.int32, shape3, 1) * KEY_VREGS + lax.broadcasted_iota(jnp.int32, shape3, 0)
           < lax.broadcasted_iota(jnp.int32, shape3, 2))
    band = functools.partial(_attn_band, tri=tri, q_cat=q_cat, k_ref=k_ref, vt_ref=vt_ref)
    n_diag = nq // KEY_TILE
    diag0 = qi * n_diag

    def with_alive(cr):
        c_max = functools.reduce(jnp.maximum, [c for _, cs in cr for c in cs])
        return (jnp.max(c_max) > 0.0).astype(jnp.int32), cr

    live, carry = lax.cond(
        qi == 0,
        lambda: with_alive(band(0, n_diag, n_diag, carry=carry)),
        lambda: with_alive(band(diag0 - BAND_BELOW, BAND_BELOW + n_diag, n_diag, carry=carry)))

    def more(state):
        kt_hi, live, _ = state
        return jnp.logical_and(kt_hi >= TAIL_TILES - 1, live > 0)

    def step(state):
        kt_hi, _, cr = state
        return (kt_hi - TAIL_TILES,) + with_alive(band(kt_hi - (TAIL_TILES - 1), TAIL_TILES, 0, carry=cr))

    _, _, carry = lax.while_loop(more, step, (diag0 - BAND_BELOW - 1, live, carry))

    o_t = jnp.concatenate([acc for acc, _ in carry], axis=0)
    o_ref[0] = o_t.T.astype(BF16)


def _attention(q, k, vt):
    bsz, seq, hd = q.shape
    n_kt = seq // KEY_TILE
    width = ATTN_LANE_BLOCKS * V7X_LANES
    assert KEY_TILE == V7X_LANES and hd % width == 0
    assert BAND_BELOW % TAIL_TILES == 0 and (Q_TILE // KEY_TILE) % TAIL_TILES == 0
    assert TAIL_TILES % SCORE_TILES == 0 and (Q_TILE // KEY_TILE) % SCORE_TILES == 0
    est = 2 * (2 * Q_TILE * width * 2 + seq * width * 2 * 2) + 16 * ATTN_LANE_BLOCKS * KEY_TILE * Q_TILE * 4
    return pl.pallas_call(
        _attn_kernel,
        grid=(bsz, hd // width, seq // Q_TILE),
        in_specs=[
            pl.BlockSpec((1, Q_TILE, width), lambda b, h, i: (b, i, h)),
            pl.BlockSpec((1, seq, width), lambda b, h, i: (b, 0, h)),
            pl.BlockSpec((1, n_kt, width, KEY_TILE), lambda b, h, i: (b, 0, h, 0)),
        ],
        out_specs=pl.BlockSpec((1, Q_TILE, width), lambda b, h, i: (b, i, h)),
        out_shape=jax.ShapeDtypeStruct((bsz, seq, hd), BF16),
        compiler_params=pltpu.CompilerParams(
            dimension_semantics=("parallel", "parallel", "arbitrary"), vmem_limit_bytes=_vmem_limit(est)),
        name="attn",
    )(q, k, vt)


def kernel(x, mix_norm, ffn_norm, final_norm, lru_w_in, lru_b_in, lru_conv_w, lru_conv_b, lru_gate_w, lru_gate_b,
           lru_lambda, lru_w_out, lru_b_out, attn_w_qkv, attn_w_o, ffn_w_in, ffn_w_out):
    depth = mix_norm.shape[0]
    assert x.shape[1] % 512 == 0 and x.shape[2] % V7X_LANES == 0
    h = x
    for layer in range(depth):
        j = layer // 2
        final_g = final_norm if layer == depth - 1 else None
        if layer % 2 == 0:
            h = _lru_block(h, mix_norm[layer], lru_w_in[j], lru_b_in[j], lru_conv_w[j], lru_conv_b[j],
                           lru_gate_w[j], lru_gate_b[j], lru_lambda[j], lru_w_out[j], lru_b_out[j])
            attn = None
        else:
            q, k, vt = _qkv_proj(h, mix_norm[layer], attn_w_qkv[j])
            attn = (_attention(q, k, vt), attn_w_o[j])
        h = _ffn_block(h, ffn_norm[layer], ffn_w_in[layer], ffn_w_out[layer], attn=attn, final_g=final_g)
    return h
```

```python
import functools
import math

import jax
import jax.numpy as jnp
from jax import lax
from jax.experimental import pallas as pl
from jax.experimental.pallas import tpu as pltpu

RMS_EPS = 1e-6
LRU_C = 8.0
HEAD_DIM = 64
N_LRU_BLOCKS = 4

V7X_LANES = 128
V7X_SUBLANES = 8
V7X_VMEM_BYTES = 64 * 1024 * 1024

ORDER_TILE = 128
ORDER_VREGS = ORDER_TILE // V7X_SUBLANES
LRU_STAGE_ROWS = 256
KEY_TILE = ORDER_TILE
KEY_VREGS = ORDER_VREGS
Q_TILE = 256
HEADS_PER_BLOCK = V7X_LANES // HEAD_DIM
ATTN_LANE_BLOCKS = 4
SCORE_TILES = 2
BAND_BELOW = 2
TAIL_TILES = 2

BF16 = jnp.bfloat16
F32 = jnp.float32


def _vmem_limit(n_bytes):
    return int(min(n_bytes * 3 // 2 + (4 << 20), V7X_VMEM_BYTES - (4 << 20)))


def _rms_norm(x, g):
    return x * lax.rsqrt(jnp.mean(x * x, axis=-1, keepdims=True) + RMS_EPS) * g


def _dot(a, b):
    return jnp.dot(a, b, preferred_element_type=F32)


def _dot_nt(a, b):
    return lax.dot_general(a, b, (((1,), (1,)), ((), ())), preferred_element_type=F32)


def _resident(shape):
    zeros = (0,) * len(shape)
    return pl.BlockSpec(shape, lambda *_: zeros, pipeline_mode=pl.Buffered(1))


def _item_of_row(p):
    return (p % V7X_SUBLANES) * ORDER_VREGS + p // V7X_SUBLANES


def _strided_perm(inverse=False):
    r = lax.broadcasted_iota(jnp.int32, (ORDER_TILE, ORDER_TILE), 0)
    c = lax.broadcasted_iota(jnp.int32, (ORDER_TILE, ORDER_TILE), 1)
    return ((r == _item_of_row(c)) if inverse else (c == _item_of_row(r))).astype(BF16)


def _to_strided_order(x16):
    perm = _strided_perm()
    tiles = [_dot(perm, x16[j * ORDER_TILE:(j + 1) * ORDER_TILE, :]) for j in range(x16.shape[0] // ORDER_TILE)]
    return jnp.concatenate(tiles, axis=0).astype(BF16)


def _shift_down(x, k, fill):
    sub = lax.broadcasted_iota(jnp.int32, x.shape, 0)
    return jnp.where(sub >= k, pltpu.roll(x, k, 0), fill)


def _vregs(x):
    return [x[v * V7X_SUBLANES:(v + 1) * V7X_SUBLANES, :] for v in range(ORDER_VREGS)]


def _lru_kernel(x_ref, g_ref, win_ref, bin_ref, cw_ref, cb_ref, gw_ref, gb_ref, lam_ref, wout_ref, bout_ref,
                o_ref, rec_tail, h_carry, *, conv_width):
    ts = x_ref.shape[1]
    w = wout_ref.shape[0]
    blk = w // N_LRU_BLOCKS
    n_stage = ts // LRU_STAGE_ROWS
    taps = conv_width - 1
    row_shape = (V7X_SUBLANES, w)

    @pl.when(pl.program_id(1) == 0)
    def _():
        rec_tail[...] = jnp.zeros_like(rec_tail)
        h_carry[...] = jnp.zeros_like(h_carry)

    perm, perm_back = _strided_perm(), _strided_perm(inverse=True)
    cw = [jnp.broadcast_to(cw_ref[k:k + 1, :], row_shape) for k in range(conv_width)]
    cb = jnp.broadcast_to(cb_ref[...], row_shape)
    sub = lax.broadcasted_iota(jnp.int32, row_shape, 0)
    nlam = -lam_ref[...]
    log_a_scale = -LRU_C * (jnp.maximum(nlam, 0.0) + jnp.log1p(jnp.exp(-jnp.abs(nlam))))

    subtiles = range(LRU_STAGE_ROWS // ORDER_TILE)

    def subtile(x, j):
        return x[j * ORDER_TILE:(j + 1) * ORDER_TILE, :]

    def permuted(p, x16):
        return jnp.concatenate([_dot(p, subtile(x16, j)) for j in subtiles], axis=0).astype(BF16)

    def in_proj(t):
        x = x_ref[0, t * LRU_STAGE_ROWS:(t + 1) * LRU_STAGE_ROWS, :]
        hn = _rms_norm(x, g_ref[...]).astype(BF16)
        u = _dot(permuted(perm, hn), win_ref[...]) + bin_ref[...]
        return u[:, :w], u[:, w:]

    def conv_gates(rec, last_rows):
        conv_rows = []
        for j in subtiles:
            r = _vregs(subtile(rec, j))
            late = [jnp.where(sub >= 1, pltpu.roll(r[ORDER_VREGS - taps + t], 1, 0), last_rows[t])
                    for t in range(taps)]
            for v in range(ORDER_VREGS):
                acc = cb
                for k in range(conv_width):
                    back = taps - k
                    acc = acc + cw[k] * (r[v - back] if v >= back else late[v - back + taps])
                conv_rows.append(acc)
            last_rows = [r[ORDER_VREGS - taps + t][V7X_SUBLANES - 1:, :] for t in range(taps)]
        conv = jnp.concatenate(conv_rows, axis=0)
        conv16 = conv.astype(BF16)
        gl = [_dot(conv16[:, n * blk:(n + 1) * blk], gw_ref[n]) for n in range(N_LRU_BLOCKS)]
        return conv, gl, last_rows

    def scan(gate_branch, conv, gl, h_in):
        gl_r = jnp.concatenate([g[:, :blk] for g in gl], axis=1) + gb_ref[0:1, :]
        gl_i = jnp.concatenate([g[:, blk:] for g in gl], axis=1) + gb_ref[1:2, :]
        a = jnp.exp(log_a_scale * jax.nn.sigmoid(gl_r))
        one_m = 1.0 - a * a
        mult = jnp.where(one_m > 0.0, one_m * lax.rsqrt(one_m), 0.0)
        b = mult * (jax.nn.sigmoid(gl_i) * conv)
        h_rows = []
        for j in subtiles:
            av, bv = _vregs(subtile(a, j)), _vregs(subtile(b, j))
            h_loc, a_run = [bv[0]], [av[0]]
            for v in range(1, ORDER_VREGS):
                h_loc.append(av[v] * h_loc[-1] + bv[v])
                a_run.append(a_run[-1] * av[v])
            a_cmp, b_cmp = a_run[-1], h_loc[-1]
            for k in (1, 2, 4):
                b_cmp = a_cmp * _shift_down(b_cmp, k, 0.0) + b_cmp
                a_cmp = a_cmp * _shift_down(a_cmp, k, 1.0)
            run_end = a_cmp * h_in + b_cmp
            run_start = _shift_down(run_end, 1, h_in)
            h_rows += [h_loc[v] + a_run[v] * run_start for v in range(ORDER_VREGS)]
            h_in = jnp.broadcast_to(run_end[V7X_SUBLANES - 1:, :], row_shape)
        y = (jax.nn.gelu(gate_branch, approximate=True) * jnp.concatenate(h_rows, axis=0)).astype(BF16)
        return y, h_in

    def out_proj(t, y):
        rows = slice(t * LRU_STAGE_ROWS, (t + 1) * LRU_STAGE_ROWS)
        o_ref[0, rows, :] = _dot(permuted(perm_back, y), wout_ref[...]) + bout_ref[...] + x_ref[0, rows, :]

    last_rows = [rec_tail[t:t + 1, :] for t in range(taps)]
    h_in = h_carry[...]
    projected, gated, scanned = {}, {}, {}
    for step in range(n_stage + 3):
        if step < n_stage:
            projected[step] = in_proj(step)
        if 0 <= step - 1 < n_stage:
            conv, gl, last_rows = conv_gates(projected[step - 1][1], last_rows)
            gated[step - 1] = (conv, gl)
        if 0 <= step - 3 < n_stage:
            out_proj(step - 3, scanned.pop(step - 3))
        if 0 <= step - 2 < n_stage:
            scanned[step - 2], h_in = scan(projected.pop(step - 2)[0], *gated.pop(step - 2), h_in)
    for t in range(taps):
        rec_tail[t:t + 1, :] = last_rows[t]
    h_carry[...] = h_in


def _lru_block(x, g, w_in, b_in, conv_w, conv_b, gate_w, gate_b, lam, w_out, b_out, *, ts=1024):
    bsz, seq, d = x.shape
    assert seq % ts == 0 and ts % LRU_STAGE_ROWS == 0 and LRU_STAGE_ROWS % ORDER_TILE == 0
    w = w_out.shape[0]
    conv_width = conv_w.shape[0]
    blk = w // N_LRU_BLOCKS
    gw = jnp.concatenate([gate_w[0], gate_w[1]], axis=-1).astype(BF16)
    row = lambda v: v.reshape(1, -1)
    est = (2 * 2 * ts * d * 4 + w_in.size * 2 + gw.size * 2 + w_out.size * 2
           + 8 * ts * w * 4)
    return pl.pallas_call(
        functools.partial(_lru_kernel, conv_width=conv_width),
        grid=(bsz, seq // ts),
        in_specs=[
            pl.BlockSpec((1, ts, d), lambda b, s: (b, s, 0)),
            _resident((1, d)),
            _resident((d, 2 * w)),
            _resident((1, 2 * w)),
            _resident((conv_width, w)),
            _resident((1, w)),
            _resident((N_LRU_BLOCKS, blk, 2 * blk)),
            _resident((2, w)),
            _resident((1, w)),
            _resident((w, d)),
            _resident((1, d)),
        ],
        out_specs=pl.BlockSpec((1, ts, d), lambda b, s: (b, s, 0)),
        out_shape=jax.ShapeDtypeStruct((bsz, seq, d), F32),
        scratch_shapes=[pltpu.VMEM((V7X_SUBLANES, w), F32), pltpu.VMEM((V7X_SUBLANES, w), F32)],
        compiler_params=pltpu.CompilerParams(
            dimension_semantics=("parallel", "arbitrary"), vmem_limit_bytes=_vmem_limit(est)),
        name="lru",
    )(x, row(g), w_in.astype(BF16), row(b_in), conv_w, row(conv_b), gw, gate_b, row(lam),
      w_out.astype(BF16), row(b_out))


def _ffn_kernel(*refs, has_attn, final, fc):
    refs = list(refs)
    x_ref = refs.pop(0)
    o_ref, wo_ref = (refs.pop(0), refs.pop(0)) if has_attn else (None, None)
    g_ref, win_ref, wout_ref = refs.pop(0), refs.pop(0), refs.pop(0)
    fg_ref = refs.pop(0) if final else None
    out_ref = refs.pop(0)

    h = x_ref[0]
    if has_attn:
        h = h + _dot(o_ref[0], wo_ref[...])
    hn = _rms_norm(h, g_ref[...]).astype(BF16)
    acc = h
    d_ff = wout_ref.shape[0]

    def gate_up(c):
        return (_dot(hn, win_ref[:, c * fc:(c + 1) * fc]),
                _dot(hn, win_ref[:, d_ff + c * fc:d_ff + (c + 1) * fc]))

    nxt = gate_up(0)
    for c in range(d_ff // fc):
        gate, up = nxt
        if (c + 1) * fc < d_ff:
            nxt = gate_up(c + 1)
        act = (gate * jax.nn.sigmoid(gate) * up).astype(BF16)
        acc = acc + _dot(act, wout_ref[c * fc:(c + 1) * fc, :])
    if final:
        acc = _rms_norm(acc, fg_ref[...])
    out_ref[0] = acc


def _ffn_block(x, g, w_in, w_out, *, attn=None, final_g=None, tm=512, fc=256):
    bsz, seq, d = x.shape
    d_ff = w_out.shape[0]
    assert d_ff % fc == 0 and fc % V7X_LANES == 0
    win = w_in.astype(BF16)
    tile = pl.BlockSpec((1, tm, d), lambda b, s: (b, s, 0))
    args, specs = [x], [tile]
    est = 2 * 2 * tm * d * 4 + (win.size + w_out.size) * 2 + 6 * tm * d * 4
    if attn is not None:
        o, w_o = attn
        args += [o, w_o.astype(BF16)]
        specs += [pl.BlockSpec((1, tm, o.shape[2]), lambda b, s: (b, s, 0)), _resident(w_o.shape)]
        est += 2 * tm * o.shape[2] * 2 + w_o.size * 2
    args += [g.reshape(1, d), win, w_out.astype(BF16)]
    specs += [_resident((1, d)), _resident(win.shape), _resident(w_out.shape)]
    if final_g is not None:
        args.append(final_g.reshape(1, d))
        specs.append(_resident((1, d)))
    return pl.pallas_call(
        functools.partial(_ffn_kernel, has_attn=attn is not None, final=final_g is not None, fc=fc),
        grid=(bsz, seq // tm),
        in_specs=specs,
        out_specs=tile,
        out_shape=jax.ShapeDtypeStruct((bsz, seq, d), F32),
        compiler_params=pltpu.CompilerParams(
            dimension_semantics=("parallel", "parallel"), vmem_limit_bytes=_vmem_limit(est)),
        name="ffn",
    )(*args)


def _qkv_kernel(x_ref, g_ref, wq_ref, wk_ref, wvt_ref, q_ref, k_ref, vt_ref, *, q_scale):
    ts = x_ref.shape[1]
    hn = _rms_norm(x_ref[0], g_ref[...]).astype(BF16)
    q_ref[0] = (_dot(hn, wq_ref[...]) * q_scale).astype(BF16)
    hp = _to_strided_order(hn)
    k_ref[0] = _dot(hp, wk_ref[...]).astype(BF16)
    vt = _dot_nt(wvt_ref[...], hp).astype(BF16)
    for j in range(ts // KEY_TILE):
        vt_ref[0, j] = vt[:, j * KEY_TILE:(j + 1) * KEY_TILE]


def _qkv_proj(x, g, w_qkv, *, ts=512):
    bsz, seq, d = x.shape
    hd = w_qkv.shape[1] // 3
    wq = w_qkv[:, :hd].astype(BF16)
    wk = w_qkv[:, hd:2 * hd].astype(BF16)
    wvt = w_qkv[:, 2 * hd:].T.astype(BF16)
    q_scale = -math.log2(math.e) / math.sqrt(HEAD_DIM)
    est = 2 * ts * d * 4 + 3 * d * hd * 2 + 2 * 3 * ts * hd * 2 + 6 * ts * d * 4
    return pl.pallas_call(
        functools.partial(_qkv_kernel, q_scale=q_scale),
        grid=(bsz, seq // ts),
        in_specs=[
            pl.BlockSpec((1, ts, d), lambda b, s: (b, s, 0)),
            _resident((1, d)), _resident((d, hd)), _resident((d, hd)), _resident((hd, d)),
        ],
        out_specs=[
            pl.BlockSpec((1, ts, hd), lambda b, s: (b, s, 0)),
            pl.BlockSpec((1, ts, hd), lambda b, s: (b, s, 0)),
            pl.BlockSpec((1, ts // KEY_TILE, hd, KEY_TILE), lambda b, s: (b, s, 0, 0)),
        ],
        out_shape=[
            jax.ShapeDtypeStruct((bsz, seq, hd), BF16),
            jax.ShapeDtypeStruct((bsz, seq, hd), BF16),
            jax.ShapeDtypeStruct((bsz, seq // KEY_TILE, hd, KEY_TILE), BF16),
        ],
        compiler_params=pltpu.CompilerParams(
            dimension_semantics=("parallel", "parallel"), vmem_limit_bytes=_vmem_limit(est)),
        name="qkv",
    )(x, g.reshape(1, d), wq, wk, wvt)


def _shift_up(x, k):
    sub = lax.broadcasted_iota(jnp.int32, x.shape, 0)
    return jnp.where(sub < V7X_SUBLANES - k, pltpu.roll(x, V7X_SUBLANES - k, 0), 1.0)


def _group_weights(s_g, c, valid):
    shape3 = (KEY_VREGS, V7X_SUBLANES, V7X_LANES)
    beta = 1.0 / (1.0 + jnp.exp2(s_g))
    f = 1.0 - beta
    beta = beta.reshape(shape3)
    f = f.reshape(shape3)
    if valid is not None:
        beta = jnp.where(valid, beta, 0.0)
        f = jnp.where(valid, f, 1.0)
    run = jnp.ones(shape3[1:], F32)
    excl = [None] * KEY_VREGS
    for v in range(KEY_VREGS - 1, -1, -1):
        excl[v] = run
        run = run * f[v]
    inc = run
    for k in (1, 2, 4):
        inc = inc * _shift_up(inc, k)
    later = _shift_up(inc, 1) * c
    c = jnp.broadcast_to(inc[0:1, :], inc.shape) * c
    wgt = jnp.concatenate([beta[v] * (excl[v] * later) for v in range(KEY_VREGS)], axis=0)
    return wgt.astype(BF16), c


def _attn_band(kt0, n_tiles, n_diag, *, tri, q_cat, k_ref, vt_ref, carry):
    n_groups = len(carry[0][1])
    nq = n_groups * V7X_LANES
    scores = {}
    for p in range(n_tiles // SCORE_TILES - 1, -1, -1):
        row0 = pl.multiple_of((kt0 + p * SCORE_TILES) * KEY_TILE, KEY_TILE)
        rows = k_ref[0, pl.ds(row0, SCORE_TILES * KEY_TILE), :]
        for lb, q_lb in enumerate(q_cat):
            scores[p, lb] = _dot_nt(rows[:, lb * V7X_LANES:(lb + 1) * V7X_LANES], q_lb)

    carry = [(acc, list(cs)) for acc, cs in carry]
    for d in range(n_tiles - 1, -1, -1):
        p, r = divmod(d, SCORE_TILES)
        j = d - (n_tiles - n_diag)
        for hi, (acc, cs) in enumerate(carry):
            lb, hh = divmod(hi, HEADS_PER_BLOCK)
            w_groups = []
            for g in range(n_groups):
                if g < j:
                    w_groups.append(jnp.zeros((KEY_TILE, V7X_LANES), BF16))
                    continue
                col0 = hh * nq + g * V7X_LANES
                s_g = scores[p, lb][r * KEY_TILE:(r + 1) * KEY_TILE, col0:col0 + V7X_LANES]
                w_g, cs[g] = _group_weights(s_g, cs[g], tri if g == j else None)
                w_groups.append(w_g)
            v_blk = vt_ref[0, kt0 + d, hi * HEAD_DIM:(hi + 1) * HEAD_DIM, :]
            carry[hi] = (acc + _dot(v_blk, jnp.concatenate(w_groups, axis=1)), cs)
    return tuple((acc, tuple(cs)) for acc, cs in carry)


def _attn_kernel(q_ref, k_ref, vt_ref, o_ref):
    nq = q_ref.shape[1]
    n_blocks = q_ref.shape[2] // V7X_LANES
    qi = pl.program_id(2)
    lane = lax.broadcasted_iota(jnp.int32, (nq, V7X_LANES), 1)
    q_cat = []
    for lb in range(n_blocks):
        q = q_ref[0, :, lb * V7X_LANES:(lb + 1) * V7X_LANES]
        q_cat.append(jnp.concatenate(
            [jnp.where(lane // HEAD_DIM == hh, q, jnp.zeros_like(q)) for hh in range(HEADS_PER_BLOCK)], axis=0))
    carry = tuple((jnp.zeros((HEAD_DIM, nq), F32),
                   tuple(jnp.ones((V7X_SUBLANES, V7X_LANES), F32) for _ in range(nq // V7X_LANES)))
                  for _ in range(n_blocks * HEADS_PER_BLOCK))
    shape3 = (KEY_VREGS, V7X_SUBLANES, V7X_LANES)
    tri = (lax.broadcasted_iota(jnp.int32, shape3, 1) * KEY_VREGS + lax.broadcasted_iota(jnp.int32, shape3, 0)
           < lax.broadcasted_iota(jnp.int32, shape3, 2))
    band = functools.partial(_attn_band, tri=tri, q_cat=q_cat, k_ref=k_ref, vt_ref=vt_ref)
    n_diag = nq // KEY_TILE
    diag0 = qi * n_diag

    def with_alive(cr):
        c_max = functools.reduce(jnp.maximum, [c for _, cs in cr for c in cs])
        return (jnp.max(c_max) > 0.0).astype(jnp.int32), cr

    live, carry = lax.cond(
        qi == 0,
        lambda: with_alive(band(0, n_diag, n_diag, carry=carry)),
        lambda: with_alive(band(diag0 - BAND_BELOW, BAND_BELOW + n_diag, n_diag, carry=carry)))

    def more(state):
        kt_hi, live, _ = state
        return jnp.logical_and(kt_hi >= TAIL_TILES - 1, live > 0)

    def step(state):
        kt_hi, _, cr = state
        return (kt_hi - TAIL_TILES,) + with_alive(band(kt_hi - (TAIL_TILES - 1), TAIL_TILES, 0, carry=cr))

    _, _, carry = lax.while_loop(more, step, (diag0 - BAND_BELOW - 1, live, carry))

    o_t = jnp.concatenate([acc for acc, _ in carry], axis=0)
    o_ref[0] = o_t.T.astype(BF16)


def _attention(q, k, vt):
    bsz, seq, hd = q.shape
    n_kt = seq // KEY_TILE
    width = ATTN_LANE_BLOCKS * V7X_LANES
    assert KEY_TILE == V7X_LANES and hd % width == 0
    assert BAND_BELOW % TAIL_TILES == 0 and (Q_TILE // KEY_TILE) % TAIL_TILES == 0
    assert TAIL_TILES % SCORE_TILES == 0 and (Q_TILE // KEY_TILE) % SCORE_TILES == 0
    est = 2 * (2 * Q_TILE * width * 2 + seq * width * 2 * 2) + 16 * ATTN_LANE_BLOCKS * KEY_TILE * Q_TILE * 4
    return pl.pallas_call(
        _attn_kernel,
        grid=(bsz, hd // width, seq // Q_TILE),
        in_specs=[
            pl.BlockSpec((1, Q_TILE, width), lambda b, h, i: (b, i, h)),
            pl.BlockSpec((1, seq, width), lambda b, h, i: (b, 0, h)),
            pl.BlockSpec((1, n_kt, width, KEY_TILE), lambda b, h, i: (b, 0, h, 0)),
        ],
        out_specs=pl.BlockSpec((1, Q_TILE, width), lambda b, h, i: (b, i, h)),
        out_shape=jax.ShapeDtypeStruct((bsz, seq, hd), BF16),
        compiler_params=pltpu.CompilerParams(
            dimension_semantics=("parallel", "parallel", "arbitrary"), vmem_limit_bytes=_vmem_limit(est)),
        name="attn",
    )(q, k, vt)


def kernel(x, mix_norm, ffn_norm, final_norm, lru_w_in, lru_b_in, lru_conv_w, lru_conv_b, lru_gate_w, lru_gate_b,
           lru_lambda, lru_w_out, lru_b_out, attn_w_qkv, attn_w_o, ffn_w_in, ffn_w_out):
    depth = mix_norm.shape[0]
    assert x.shape[1] % 512 == 0 and x.shape[2] % V7X_LANES == 0
    h = x
    for layer in range(depth):
        j = layer // 2
        final_g = final_norm if layer == depth - 1 else None
        if layer % 2 == 0:
            h = _lru_block(h, mix_norm[layer], lru_w_in[j], lru_b_in[j], lru_conv_w[j], lru_conv_b[j],
                           lru_gate_w[j], lru_gate_b[j], lru_lambda[j], lru_w_out[j], lru_b_out[j])
            attn = None
        else:
            q, k, vt = _qkv_proj(h, mix_norm[layer], attn_w_qkv[j])
            attn = (_attention(q, k, vt), attn_w_o[j])
        h = _ffn_block(h, ffn_norm[layer], ffn_w_in[layer], ffn_w_out[layer], attn=attn, final_g=final_g)
    return h
```

```python
import functools
import math

import jax
import jax.numpy as jnp
from jax import lax
from jax.experimental import pallas as pl
from jax.experimental.pallas import tpu as pltpu

RMS_EPS = 1e-6
LRU_C = 8.0
HEAD_DIM = 64
N_LRU_BLOCKS = 4

V7X_LANES = 128
V7X_SUBLANES = 8
V7X_VMEM_BYTES = 64 * 1024 * 1024

ORDER_TILE = 128
ORDER_VREGS = ORDER_TILE // V7X_SUBLANES
LRU_STAGE_ROWS = 256
KEY_TILE = ORDER_TILE
KEY_VREGS = ORDER_VREGS
Q_TILE = 256
HEADS_PER_BLOCK = V7X_LANES // HEAD_DIM
ATTN_LANE_BLOCKS = 4
SCORE_TILES = 2
BAND_BELOW = 2
TAIL_TILES = 2

BF16 = jnp.bfloat16
F32 = jnp.float32


def _vmem_limit(n_bytes):
    return int(min(n_bytes * 3 // 2 + (4 << 20), V7X_VMEM_BYTES - (4 << 20)))


def _rms_norm(x, g):
    return x * lax.rsqrt(jnp.mean(x * x, axis=-1, keepdims=True) + RMS_EPS) * g


def _dot(a, b):
    return jnp.dot(a, b, preferred_element_type=F32)


def _dot_nt(a, b):
    return lax.dot_general(a, b, (((1,), (1,)), ((), ())), preferred_element_type=F32)


def _resident(shape):
    zeros = (0,) * len(shape)
    return pl.BlockSpec(shape, lambda *_: zeros, pipeline_mode=pl.Buffered(1))


def _item_of_row(p):
    return (p % V7X_SUBLANES) * ORDER_VREGS + p // V7X_SUBLANES


def _strided_perm(inverse=False):
    r = lax.broadcasted_iota(jnp.int32, (ORDER_TILE, ORDER_TILE), 0)
    c = lax.broadcasted_iota(jnp.int32, (ORDER_TILE, ORDER_TILE), 1)
    return ((r == _item_of_row(c)) if inverse else (c == _item_of_row(r))).astype(BF16)


def _to_strided_order(x16):
    perm = _strided_perm()
    tiles = [_dot(perm, x16[j * ORDER_TILE:(j + 1) * ORDER_TILE, :]) for j in range(x16.shape[0] // ORDER_TILE)]
    return jnp.concatenate(tiles, axis=0).astype(BF16)


def _shift_down(x, k, fill):
    sub = lax.broadcasted_iota(jnp.int32, x.shape, 0)
    return jnp.where(sub >= k, pltpu.roll(x, k, 0), fill)


def _vregs(x):
    return [x[v * V7X_SUBLANES:(v + 1) * V7X_SUBLANES, :] for v in range(ORDER_VREGS)]


def _lru_kernel(x_ref, g_ref, win_ref, bin_ref, cw_ref, cb_ref, gw_ref, gb_ref, lam_ref, wout_ref, bout_ref,
                o_ref, rec_tail, h_carry, *, conv_width):
    ts = x_ref.shape[1]
    w = wout_ref.shape[0]
    blk = w // N_LRU_BLOCKS
    n_stage = ts // LRU_STAGE_ROWS
    taps = conv_width - 1
    row_shape = (V7X_SUBLANES, w)

    @pl.when(pl.program_id(1) == 0)
    def _():
        rec_tail[...] = jnp.zeros_like(rec_tail)
        h_carry[...] = jnp.zeros_like(h_carry)

    perm, perm_back = _strided_perm(), _strided_perm(inverse=True)
    cw = [jnp.broadcast_to(cw_ref[k:k + 1, :], row_shape) for k in range(conv_width)]
    cb = jnp.broadcast_to(cb_ref[...], row_shape)
    sub = lax.broadcasted_iota(jnp.int32, row_shape, 0)
    nlam = -lam_ref[...]
    log_a_scale = -LRU_C * (jnp.maximum(nlam, 0.0) + jnp.log1p(jnp.exp(-jnp.abs(nlam))))

    subtiles = range(LRU_STAGE_ROWS // ORDER_TILE)

    def subtile(x, j):
        return x[j * ORDER_TILE:(j + 1) * ORDER_TILE, :]

    def permuted(p, x16):
        return jnp.concatenate([_dot(p, subtile(x16, j)) for j in subtiles], axis=0).astype(BF16)

    def in_proj(t):
        x = x_ref[0, t * LRU_STAGE_ROWS:(t + 1) * LRU_STAGE_ROWS, :]
        hn = _rms_norm(x, g_ref[...]).astype(BF16)
        u = _dot(permuted(perm, hn), win_ref[...]) + bin_ref[...]
        return u[:, :w], u[:, w:]

    def conv_gates(rec, last_rows):
        conv_rows = []
        for j in subtiles:
            r = _vregs(subtile(rec, j))
            late = [jnp.where(sub >= 1, pltpu.roll(r[ORDER_VREGS - taps + t], 1, 0), last_rows[t])
                    for t in range(taps)]
            for v in range(ORDER_VREGS):
                acc = cb
                for k in range(conv_width):
                    back = taps - k
                    acc = acc + cw[k] * (r[v - back] if v >= back else late[v - back + taps])
                conv_rows.append(acc)
            last_rows = [r[ORDER_VREGS - taps + t][V7X_SUBLANES - 1:, :] for t in range(taps)]
        conv = jnp.concatenate(conv_rows, axis=0)
        conv16 = conv.astype(BF16)
        gl = [_dot(conv16[:, n * blk:(n + 1) * blk], gw_ref[n]) for n in range(N_LRU_BLOCKS)]
        return conv, gl, last_rows

    def scan(gate_branch, conv, gl, h_in):
        gl_r = jnp.concatenate([g[:, :blk] for g in gl], axis=1) + gb_ref[0:1, :]
        gl_i = jnp.concatenate([g[:, blk:] for g in gl], axis=1) + gb_ref[1:2, :]
        a = jnp.exp(log_a_scale * jax.nn.sigmoid(gl_r))
        one_m = 1.0 - a * a
        mult = jnp.where(one_m > 0.0, one_m * lax.rsqrt(one_m), 0.0)
        b = mult * (jax.nn.sigmoid(gl_i) * conv)
        h_rows = []
        for j in subtiles:
            av, bv = _vregs(subtile(a, j)), _vregs(subtile(b, j))
            h_loc, a_run = [bv[0]], [av[0]]
            for v in range(1, ORDER_VREGS):
                h_loc.append(av[v] * h_loc[-1] + bv[v])
                a_run.append(a_run[-1] * av[v])
            a_cmp, b_cmp = a_run[-1], h_loc[-1]
            for k in (1, 2, 4):
                b_cmp = a_cmp * _shift_down(b_cmp, k, 0.0) + b_cmp
                a_cmp = a_cmp * _shift_down(a_cmp, k, 1.0)
            run_end = a_cmp * h_in + b_cmp
            run_start = _shift_down(run_end, 1, h_in)
            h_rows += [h_loc[v] + a_run[v] * run_start for v in range(ORDER_VREGS)]
            h_in = jnp.broadcast_to(run_end[V7X_SUBLANES - 1:, :], row_shape)
        y = (jax.nn.gelu(gate_branch, approximate=True) * jnp.concatenate(h_rows, axis=0)).astype(BF16)
        return y, h_in

    def out_proj(t, y):
        rows = slice(t * LRU_STAGE_ROWS, (t + 1) * LRU_STAGE_ROWS)
        o_ref[0, rows, :] = _dot(permuted(perm_back, y), wout_ref[...]) + bout_ref[...] + x_ref[0, rows, :]

    last_rows = [rec_tail[t:t + 1, :] for t in range(taps)]
    h_in = h_carry[...]
    projected, gated, scanned = {}, {}, {}
    for step in range(n_stage + 3):
        if step < n_stage:
            projected[step] = in_proj(step)
        if 0 <= step - 1 < n_stage:
            conv, gl, last_rows = conv_gates(projected[step - 1][1], last_rows)
            gated[step - 1] = (conv, gl)
        if 0 <= step - 3 < n_stage:
            out_proj(step - 3, scanned.pop(step - 3))
        if 0 <= step - 2 < n_stage:
            scanned[step - 2], h_in = scan(projected.pop(step - 2)[0], *gated.pop(step - 2), h_in)
    for t in range(taps):
        rec_tail[t:t + 1, :] = last_rows[t]
    h_carry[...] = h_in


def _lru_block(x, g, w_in, b_in, conv_w, conv_b, gate_w, gate_b, lam, w_out, b_out, *, ts=1024):
    bsz, seq, d = x.shape
    assert seq % ts == 0 and ts % LRU_STAGE_ROWS == 0 and LRU_STAGE_ROWS % ORDER_TILE == 0
    w = w_out.shape[0]
    conv_width = conv_w.shape[0]
    blk = w // N_LRU_BLOCKS
    gw = jnp.concatenate([gate_w[0], gate_w[1]], axis=-1).astype(BF16)
    row = lambda v: v.reshape(1, -1)
    est = (2 * 2 * ts * d * 4 + w_in.size * 2 + gw.size * 2 + w_out.size * 2
           + 8 * ts * w * 4)
    return pl.pallas_call(
        functools.partial(_lru_kernel, conv_width=conv_width),
        grid=(bsz, seq // ts),
        in_specs=[
            pl.BlockSpec((1, ts, d), lambda b, s: (b, s, 0)),
            _resident((1, d)),
            _resident((d, 2 * w)),
            _resident((1, 2 * w)),
            _resident((conv_width, w)),
            _resident((1, w)),
            _resident((N_LRU_BLOCKS, blk, 2 * blk)),
            _resident((2, w)),
            _resident((1, w)),
            _resident((w, d)),
            _resident((1, d)),
        ],
        out_specs=pl.BlockSpec((1, ts, d), lambda b, s: (b, s, 0)),
        out_shape=jax.ShapeDtypeStruct((bsz, seq, d), F32),
        scratch_shapes=[pltpu.VMEM((V7X_SUBLANES, w), F32), pltpu.VMEM((V7X_SUBLANES, w), F32)],
        compiler_params=pltpu.CompilerParams(
            dimension_semantics=("parallel", "arbitrary"), vmem_limit_bytes=_vmem_limit(est)),
        name="lru",
    )(x, row(g), w_in.astype(BF16), row(b_in), conv_w, row(conv_b), gw, gate_b, row(lam),
      w_out.astype(BF16), row(b_out))


def _ffn_kernel(*refs, has_attn, final, fc):
    refs = list(refs)
    x_ref = refs.pop(0)
    o_ref, wo_ref = (refs.pop(0), refs.pop(0)) if has_attn else (None, None)
    g_ref, win_ref, wout_ref = refs.pop(0), refs.pop(0), refs.pop(0)
    fg_ref = refs.pop(0) if final else None
    out_ref = refs.pop(0)

    h = x_ref[0]
    if has_attn:
        h = h + _dot(o_ref[0], wo_ref[...])
    hn = _rms_norm(h, g_ref[...]).astype(BF16)
    acc = h
    d_ff = wout_ref.shape[0]

    def gate_up(c):
        return (_dot(hn, win_ref[:, c * fc:(c + 1) * fc]),
                _dot(hn, win_ref[:, d_ff + c * fc:d_ff + (c + 1) * fc]))

    nxt = gate_up(0)
    for c in range(d_ff // fc):
        gate, up = nxt
        if (c + 1) * fc < d_ff:
            nxt = gate_up(c + 1)
        act = (gate * jax.nn.sigmoid(gate) * up).astype(BF16)
        acc = acc + _dot(act, wout_ref[c * fc:(c + 1) * fc, :])
    if final:
        acc = _rms_norm(acc, fg_ref[...])
    out_ref[0] = acc


def _ffn_block(x, g, w_in, w_out, *, attn=None, final_g=None, tm=512, fc=256):
    bsz, seq, d = x.shape
    d_ff = w_out.shape[0]
    assert d_ff % fc == 0 and fc % V7X_LANES == 0
    win = w_in.astype(BF16)
    tile = pl.BlockSpec((1, tm, d), lambda b, s: (b, s, 0))
    args, specs = [x], [tile]
    est = 2 * 2 * tm * d * 4 + (win.size + w_out.size) * 2 + 6 * tm * d * 4
    if attn is not None:
        o, w_o = attn
        args += [o, w_o.astype(BF16)]
        specs += [pl.BlockSpec((1, tm, o.shape[2]), lambda b, s: (b, s, 0)), _resident(w_o.shape)]
        est += 2 * tm * o.shape[2] * 2 + w_o.size * 2
    args += [g.reshape(1, d), win, w_out.astype(BF16)]
    specs += [_resident((1, d)), _resident(win.shape), _resident(w_out.shape)]
    if final_g is not None:
        args.append(final_g.reshape(1, d))
        specs.append(_resident((1, d)))
    return pl.pallas_call(
        functools.partial(_ffn_kernel, has_attn=attn is not None, final=final_g is not None, fc=fc),
        grid=(bsz, seq // tm),
        in_specs=specs,
        out_specs=tile,
        out_shape=jax.ShapeDtypeStruct((bsz, seq, d), F32),
        compiler_params=pltpu.CompilerParams(
            dimension_semantics=("parallel", "parallel"), vmem_limit_bytes=_vmem_limit(est)),
        name="ffn",
    )(*args)


def _qkv_kernel(x_ref, g_ref, wq_ref, wk_ref, wvt_ref, q_ref, k_ref, vt_ref, *, q_scale):
    ts = x_ref.shape[1]
    hn = _rms_norm(x_ref[0], g_ref[...]).astype(BF16)
    q_ref[0] = (_dot(hn, wq_ref[...]) * q_scale).astype(BF16)
    hp = _to_strided_order(hn)
    k_ref[0] = _dot(hp, wk_ref[...]).astype(BF16)
    vt = _dot_nt(wvt_ref[...], hp).astype(BF16)
    for j in range(ts // KEY_TILE):
        vt_ref[0, j] = vt[:, j * KEY_TILE:(j + 1) * KEY_TILE]


def _qkv_proj(x, g, w_qkv, *, ts=512):
    bsz, seq, d = x.shape
    hd = w_qkv.shape[1] // 3
    wq = w_qkv[:, :hd].astype(BF16)
    wk = w_qkv[:, hd:2 * hd].astype(BF16)
    wvt = w_qkv[:, 2 * hd:].T.astype(BF16)
    q_scale = 0.5 / math.sqrt(HEAD_DIM)
    est = 2 * ts * d * 4 + 3 * d * hd * 2 + 2 * 3 * ts * hd * 2 + 6 * ts * d * 4
    return pl.pallas_call(
        functools.partial(_qkv_kernel, q_scale=q_scale),
        grid=(bsz, seq // ts),
        in_specs=[
            pl.BlockSpec((1, ts, d), lambda b, s: (b, s, 0)),
            _resident((1, d)), _resident((d, hd)), _resident((d, hd)), _resident((hd, d)),
        ],
        out_specs=[
            pl.BlockSpec((1, ts, hd), lambda b, s: (b, s, 0)),
            pl.BlockSpec((1, ts, hd), lambda b, s: (b, s, 0)),
            pl.BlockSpec((1, ts // KEY_TILE, hd, KEY_TILE), lambda b, s: (b, s, 0, 0)),
        ],
        out_shape=[
            jax.ShapeDtypeStruct((bsz, seq, hd), BF16),
            jax.ShapeDtypeStruct((bsz, seq, hd), BF16),
            jax.ShapeDtypeStruct((bsz, seq // KEY_TILE, hd, KEY_TILE), BF16),
        ],
        compiler_params=pltpu.CompilerParams(
            dimension_semantics=("parallel", "parallel"), vmem_limit_bytes=_vmem_limit(est)),
        name="qkv",
    )(x, g.reshape(1, d), wq, wk, wvt)


def _shift_up(x, k):
    sub = lax.broadcasted_iota(jnp.int32, x.shape, 0)
    return jnp.where(sub < V7X_SUBLANES - k, pltpu.roll(x, V7X_SUBLANES - k, 0), 1.0)


def _group_weights(s_g, c, valid):
    shape3 = (KEY_VREGS, V7X_SUBLANES, V7X_LANES)
    beta = (0.5 + 0.5 * jnp.tanh(s_g)).reshape(shape3)
    if valid is not None:
        beta = jnp.where(valid, beta, 0.0)
    run = jnp.ones(shape3[1:], F32)
    part = [None] * KEY_VREGS
    for v in range(KEY_VREGS - 1, -1, -1):
        part[v] = beta[v] * run
        run = run - part[v]
    inc = run
    for k in (1, 2, 4):
        inc = inc * _shift_up(inc, k)
    later = _shift_up(inc, 1) * c
    c = jnp.broadcast_to(inc[0:1, :], inc.shape) * c
    wgt = jnp.concatenate([part[v] * later for v in range(KEY_VREGS)], axis=0)
    return wgt.astype(BF16), c


def _attn_band(kt0, n_tiles, n_diag, *, tri, q_cat, k_ref, vt_ref, carry):
    n_groups = len(carry[0][1])
    nq = n_groups * V7X_LANES
    scores = {}
    for p in range(n_tiles // SCORE_TILES - 1, -1, -1):
        row0 = pl.multiple_of((kt0 + p * SCORE_TILES) * KEY_TILE, KEY_TILE)
        rows = k_ref[0, pl.ds(row0, SCORE_TILES * KEY_TILE), :]
        for lb, q_lb in enumerate(q_cat):
            scores[p, lb] = _dot_nt(rows[:, lb * V7X_LANES:(lb + 1) * V7X_LANES], q_lb)

    carry = [(acc, list(cs)) for acc, cs in carry]
    for d in range(n_tiles - 1, -1, -1):
        p, r = divmod(d, SCORE_TILES)
        j = d - (n_tiles - n_diag)
        for hi, (acc, cs) in enumerate(carry):
            lb, hh = divmod(hi, HEADS_PER_BLOCK)
            w_groups = []
            for g in range(n_groups):
                if g < j:
                    w_groups.append(jnp.zeros((KEY_TILE, V7X_LANES), BF16))
                    continue
                col0 = hh * nq + g * V7X_LANES
                s_g = scores[p, lb][r * KEY_TILE:(r + 1) * KEY_TILE, col0:col0 + V7X_LANES]
                w_g, cs[g] = _group_weights(s_g, cs[g], tri if g == j else None)
                w_groups.append(w_g)
            v_blk = vt_ref[0, kt0 + d, hi * HEAD_DIM:(hi + 1) * HEAD_DIM, :]
            carry[hi] = (acc + _dot(v_blk, jnp.concatenate(w_groups, axis=1)), cs)
    return tuple((acc, tuple(cs)) for acc, cs in carry)


def _attn_kernel(q_ref, k_ref, vt_ref, o_ref):
    nq = q_ref.shape[1]
    n_blocks = q_ref.shape[2] // V7X_LANES
    qi = pl.program_id(2)
    lane = lax.broadcasted_iota(jnp.int32, (nq, V7X_LANES), 1)
    q_cat = []
    for lb in range(n_blocks):
        q = q_ref[0, :, lb * V7X_LANES:(lb + 1) * V7X_LANES]
        q_cat.append(jnp.concatenate(
            [jnp.where(lane // HEAD_DIM == hh, q, jnp.zeros_like(q)) for hh in range(HEADS_PER_BLOCK)], axis=0))
    carry = tuple((jnp.zeros((HEAD_DIM, nq), F32),
                   tuple(jnp.ones((V7X_SUBLANES, V7X_LANES), F32) for _ in range(nq // V7X_LANES)))
                  for _ in range(n_blocks * HEADS_PER_BLOCK))
    shape3 = (KEY_VREGS, V7X_SUBLANES, V7X_LANES)
    tri = (lax.broadcasted_iota(jnp.int32, shape3, 1) * KEY_VREGS + lax.broadcasted_iota(jnp.int32, shape3, 0)
           < lax.broadcasted_iota(jnp.int32, shape3, 2))
    band = functools.partial(_attn_band, tri=tri, q_cat=q_cat, k_ref=k_ref, vt_ref=vt_ref)
    n_diag = nq // KEY_TILE
    diag0 = qi * n_diag

    def with_alive(cr):
        c_max = functools.reduce(jnp.maximum, [c for _, cs in cr for c in cs])
        return (jnp.max(c_max) > 0.0).astype(jnp.int32), cr

    live, carry = lax.cond(
        qi == 0,
        lambda: with_alive(band(0, n_diag, n_diag, carry=carry)),
        lambda: with_alive(band(diag0 - BAND_BELOW, BAND_BELOW + n_diag, n_diag, carry=carry)))

    def more(state):
        kt_hi, live, _ = state
        return jnp.logical_and(kt_hi >= TAIL_TILES - 1, live > 0)

    def step(state):
        kt_hi, _, cr = state
        return (kt_hi - TAIL_TILES,) + with_alive(band(kt_hi - (TAIL_TILES - 1), TAIL_TILES, 0, carry=cr))

    _, _, carry = lax.while_loop(more, step, (diag0 - BAND_BELOW - 1, live, carry))

    o_t = jnp.concatenate([acc for acc, _ in carry], axis=0)
    o_ref[0] = o_t.T.astype(BF16)


def _attention(q, k, vt):
    bsz, seq, hd = q.shape
    n_kt = seq // KEY_TILE
    width = ATTN_LANE_BLOCKS * V7X_LANES
    assert KEY_TILE == V7X_LANES and hd % width == 0
    assert BAND_BELOW % TAIL_TILES == 0 and (Q_TILE // KEY_TILE) % TAIL_TILES == 0
    assert TAIL_TILES % SCORE_TILES == 0 and (Q_TILE // KEY_TILE) % SCORE_TILES == 0
    est = 2 * (2 * Q_TILE * width * 2 + seq * width * 2 * 2) + 16 * ATTN_LANE_BLOCKS * KEY_TILE * Q_TILE * 4
    return pl.pallas_call(
        _attn_kernel,
        grid=(bsz, hd // width, seq // Q_TILE),
        in_specs=[
            pl.BlockSpec((1, Q_TILE, width), lambda b, h, i: (b, i, h)),
            pl.BlockSpec((1, seq, width), lambda b, h, i: (b, 0, h)),
            pl.BlockSpec((1, n_kt, width, KEY_TILE), lambda b, h, i: (b, 0, h, 0)),
        ],
        out_specs=pl.BlockSpec((1, Q_TILE, width), lambda b, h, i: (b, i, h)),
        out_shape=jax.ShapeDtypeStruct((bsz, seq, hd), BF16),
        compiler_params=pltpu.CompilerParams(
            dimension_semantics=("parallel", "parallel", "arbitrary"), vmem_limit_bytes=_vmem_limit(est)),
        name="attn",
    )(q, k, vt)


def kernel(x, mix_norm, ffn_norm, final_norm, lru_w_in, lru_b_in, lru_conv_w, lru_conv_b, lru_gate_w, lru_gate_b,
           lru_lambda, lru_w_out, lru_b_out, attn_w_qkv, attn_w_o, ffn_w_in, ffn_w_out):
    depth = mix_norm.shape[0]
    assert x.shape[1] % 512 == 0 and x.shape[2] % V7X_LANES == 0
    h = x
    for layer in range(depth):
        j = layer // 2
        final_g = final_norm if layer == depth - 1 else None
        if layer % 2 == 0:
            h = _lru_block(h, mix_norm[layer], lru_w_in[j], lru_b_in[j], lru_conv_w[j], lru_conv_b[j],
                           lru_gate_w[j], lru_gate_b[j], lru_lambda[j], lru_w_out[j], lru_b_out[j])
            attn = None
        else:
            q, k, vt = _qkv_proj(h, mix_norm[layer], attn_w_qkv[j])
            attn = (_attention(q, k, vt), attn_w_o[j])
        h = _ffn_block(h, ffn_norm[layer], ffn_w_in[layer], ffn_w_out[layer], attn=attn, final_g=final_g)
    return h
```

```python
import functools
import math

import jax
import jax.numpy as jnp
from jax import lax
from jax.experimental import pallas as pl
from jax.experimental.pallas import tpu as pltpu

RMS_EPS = 1e-6
LRU_C = 8.0
HEAD_DIM = 64
N_LRU_BLOCKS = 4

V7X_LANES = 128
V7X_SUBLANES = 8
V7X_VMEM_BYTES = 64 * 1024 * 1024

ORDER_TILE = 128
ORDER_VREGS = ORDER_TILE // V7X_SUBLANES
LRU_STAGE_ROWS = 256
KEY_TILE = ORDER_TILE
KEY_VREGS = ORDER_VREGS
Q_TILE = 256
HEADS_PER_BLOCK = V7X_LANES // HEAD_DIM
ATTN_LANE_BLOCKS = 4
SCORE_TILES = 2
BAND_BELOW = 2

BF16 = jnp.bfloat16
F32 = jnp.float32


def _vmem_limit(n_bytes):
    return int(min(n_bytes * 3 // 2 + (4 << 20), V7X_VMEM_BYTES - (4 << 20)))


def _rms_norm(x, g):
    return x * lax.rsqrt(jnp.mean(x * x, axis=-1, keepdims=True) + RMS_EPS) * g


def _dot(a, b):
    return jnp.dot(a, b, preferred_element_type=F32)


def _dot_nt(a, b):
    return lax.dot_general(a, b, (((1,), (1,)), ((), ())), preferred_element_type=F32)


def _resident(shape):
    zeros = (0,) * len(shape)
    return pl.BlockSpec(shape, lambda *_: zeros, pipeline_mode=pl.Buffered(1))


def _item_of_row(p):
    return (p % V7X_SUBLANES) * ORDER_VREGS + p // V7X_SUBLANES


def _strided_perm(inverse=False):
    r = lax.broadcasted_iota(jnp.int32, (ORDER_TILE, ORDER_TILE), 0)
    c = lax.broadcasted_iota(jnp.int32, (ORDER_TILE, ORDER_TILE), 1)
    return ((r == _item_of_row(c)) if inverse else (c == _item_of_row(r))).astype(BF16)


def _to_strided_order(x16):
    perm = _strided_perm()
    tiles = [_dot(perm, x16[j * ORDER_TILE:(j + 1) * ORDER_TILE, :]) for j in range(x16.shape[0] // ORDER_TILE)]
    return jnp.concatenate(tiles, axis=0).astype(BF16)


def _shift_down(x, k, fill):
    sub = lax.broadcasted_iota(jnp.int32, x.shape, 0)
    return jnp.where(sub >= k, pltpu.roll(x, k, 0), fill)


def _vregs(x):
    return [x[v * V7X_SUBLANES:(v + 1) * V7X_SUBLANES, :] for v in range(ORDER_VREGS)]


def _lru_kernel(x_ref, g_ref, win_ref, bin_ref, cw_ref, cb_ref, gw_ref, gb_ref, lam_ref, wout_ref, bout_ref,
                o_ref, rec_tail, h_carry, *, conv_width):
    ts = x_ref.shape[1]
    w = wout_ref.shape[0]
    blk = w // N_LRU_BLOCKS
    n_stage = ts // LRU_STAGE_ROWS
    taps = conv_width - 1
    row_shape = (V7X_SUBLANES, w)

    @pl.when(pl.program_id(1) == 0)
    def _():
        rec_tail[...] = jnp.zeros_like(rec_tail)
        h_carry[...] = jnp.zeros_like(h_carry)

    perm, perm_back = _strided_perm(), _strided_perm(inverse=True)
    cw = [jnp.broadcast_to(cw_ref[k:k + 1, :], row_shape) for k in range(conv_width)]
    cb = jnp.broadcast_to(cb_ref[...], row_shape)
    sub = lax.broadcasted_iota(jnp.int32, row_shape, 0)
    nlam = -lam_ref[...]
    log_a_scale = -LRU_C * (jnp.maximum(nlam, 0.0) + jnp.log1p(jnp.exp(-jnp.abs(nlam))))

    subtiles = range(LRU_STAGE_ROWS // ORDER_TILE)

    def subtile(x, j):
        return x[j * ORDER_TILE:(j + 1) * ORDER_TILE, :]

    def permuted(p, x16):
        return jnp.concatenate([_dot(p, subtile(x16, j)) for j in subtiles], axis=0).astype(BF16)

    def normed(t):
        x = x_ref[0, t * LRU_STAGE_ROWS:(t + 1) * LRU_STAGE_ROWS, :]
        return permuted(perm, _rms_norm(x, g_ref[...]).astype(BF16))

    hn = [normed(t) for t in range(n_stage)]

    def in_proj(t):
        u = _dot(hn[t], win_ref[...]) + bin_ref[...]
        return u[:, :w], u[:, w:]

    def conv_gates(rec, last_rows):
        conv_rows = []
        for j in subtiles:
            r = _vregs(subtile(rec, j))
            late = [jnp.where(sub >= 1, pltpu.roll(r[ORDER_VREGS - taps + t], 1, 0), last_rows[t])
                    for t in range(taps)]
            for v in range(ORDER_VREGS):
                acc = cb
                for k in range(conv_width):
                    back = taps - k
                    acc = acc + cw[k] * (r[v - back] if v >= back else late[v - back + taps])
                conv_rows.append(acc)
            last_rows = [r[ORDER_VREGS - taps + t][V7X_SUBLANES - 1:, :] for t in range(taps)]
        conv = jnp.concatenate(conv_rows, axis=0)
        conv16 = conv.astype(BF16)
        gl = [_dot(conv16[:, n * blk:(n + 1) * blk], gw_ref[n]) for n in range(N_LRU_BLOCKS)]
        return conv, gl, last_rows

    def scan(gate_branch, conv, gl, h_in):
        gl_r = jnp.concatenate([g[:, :blk] for g in gl], axis=1) + gb_ref[0:1, :]
        gl_i = jnp.concatenate([g[:, blk:] for g in gl], axis=1) + gb_ref[1:2, :]
        a = jnp.exp(log_a_scale * jax.nn.sigmoid(gl_r))
        one_m = 1.0 - a * a
        mult = jnp.where(one_m > 0.0, one_m * lax.rsqrt(one_m), 0.0)
        b = mult * (jax.nn.sigmoid(gl_i) * conv)
        h_rows = []
        for j in subtiles:
            av, bv = _vregs(subtile(a, j)), _vregs(subtile(b, j))
            h_loc, a_run = [bv[0]], [av[0]]
            for v in range(1, ORDER_VREGS):
                h_loc.append(av[v] * h_loc[-1] + bv[v])
                a_run.append(a_run[-1] * av[v])
            a_cmp, b_cmp = a_run[-1], h_loc[-1]
            for k in (1, 2, 4):
                b_cmp = a_cmp * _shift_down(b_cmp, k, 0.0) + b_cmp
                a_cmp = a_cmp * _shift_down(a_cmp, k, 1.0)
            run_end = a_cmp * h_in + b_cmp
            run_start = _shift_down(run_end, 1, h_in)
            h_rows += [h_loc[v] + a_run[v] * run_start for v in range(ORDER_VREGS)]
            h_in = jnp.broadcast_to(run_end[V7X_SUBLANES - 1:, :], row_shape)
        y = (jax.nn.gelu(gate_branch, approximate=True) * jnp.concatenate(h_rows, axis=0)).astype(BF16)
        return y, h_in

    def out_proj(t, y):
        rows = slice(t * LRU_STAGE_ROWS, (t + 1) * LRU_STAGE_ROWS)
        o_ref[0, rows, :] = _dot(permuted(perm_back, y), wout_ref[...]) + bout_ref[...] + x_ref[0, rows, :]

    last_rows = [rec_tail[t:t + 1, :] for t in range(taps)]
    h_in = h_carry[...]
    projected, gated, scanned = {}, {}, {}
    for step in range(n_stage + 3):
        if step < n_stage:
            projected[step] = in_proj(step)
        if 0 <= step - 1 < n_stage:
            conv, gl, last_rows = conv_gates(projected[step - 1][1], last_rows)
            gated[step - 1] = (conv, gl)
        if 0 <= step - 3 < n_stage:
            out_proj(step - 3, scanned.pop(step - 3))
        if 0 <= step - 2 < n_stage:
            scanned[step - 2], h_in = scan(projected.pop(step - 2)[0], *gated.pop(step - 2), h_in)
    for t in range(taps):
        rec_tail[t:t + 1, :] = last_rows[t]
    h_carry[...] = h_in


def _lru_block(x, g, w_in, b_in, conv_w, conv_b, gate_w, gate_b, lam, w_out, b_out, *, ts=1024):
    bsz, seq, d = x.shape
    assert seq % ts == 0 and ts % LRU_STAGE_ROWS == 0 and LRU_STAGE_ROWS % ORDER_TILE == 0
    w = w_out.shape[0]
    conv_width = conv_w.shape[0]
    blk = w // N_LRU_BLOCKS
    gw = jnp.concatenate([gate_w[0], gate_w[1]], axis=-1).astype(BF16)
    row = lambda v: v.reshape(1, -1)
    est = (2 * 2 * ts * d * 4 + w_in.size * 2 + gw.size * 2 + w_out.size * 2
           + 8 * ts * w * 4)
    return pl.pallas_call(
        functools.partial(_lru_kernel, conv_width=conv_width),
        grid=(bsz, seq // ts),
        in_specs=[
            pl.BlockSpec((1, ts, d), lambda b, s: (b, s, 0)),
            _resident((1, d)),
            _resident((d, 2 * w)),
            _resident((1, 2 * w)),
            _resident((conv_width, w)),
            _resident((1, w)),
            _resident((N_LRU_BLOCKS, blk, 2 * blk)),
            _resident((2, w)),
            _resident((1, w)),
            _resident((w, d)),
            _resident((1, d)),
        ],
        out_specs=pl.BlockSpec((1, ts, d), lambda b, s: (b, s, 0)),
        out_shape=jax.ShapeDtypeStruct((bsz, seq, d), F32),
        scratch_shapes=[pltpu.VMEM((V7X_SUBLANES, w), F32), pltpu.VMEM((V7X_SUBLANES, w), F32)],
        compiler_params=pltpu.CompilerParams(
            dimension_semantics=("parallel", "arbitrary"), vmem_limit_bytes=_vmem_limit(est)),
        name="lru",
    )(x, row(g), w_in.astype(BF16), row(b_in), conv_w, row(conv_b), gw, gate_b, row(lam),
      w_out.astype(BF16), row(b_out))


def _ffn_kernel(*refs, has_attn, final, fc):
    refs = list(refs)
    x_ref = refs.pop(0)
    o_ref, wo_ref = (refs.pop(0), refs.pop(0)) if has_attn else (None, None)
    g_ref, win_ref, wout_ref = refs.pop(0), refs.pop(0), refs.pop(0)
    fg_ref = refs.pop(0) if final else None
    out_ref = refs.pop(0)

    h = x_ref[0]
    if has_attn:
        h = h + _dot(o_ref[0], wo_ref[...])
    hn = _rms_norm(h, g_ref[...]).astype(BF16)
    acc = h
    d_ff = wout_ref.shape[0]

    def gate_up(c):
        return (_dot(hn, win_ref[:, c * fc:(c + 1) * fc]),
                _dot(hn, win_ref[:, d_ff + c * fc:d_ff + (c + 1) * fc]))

    nxt = gate_up(0)
    for c in range(d_ff // fc):
        gate, up = nxt
        if (c + 1) * fc < d_ff:
            nxt = gate_up(c + 1)
        act = (gate * jax.nn.sigmoid(gate) * up).astype(BF16)
        acc = acc + _dot(act, wout_ref[c * fc:(c + 1) * fc, :])
    if final:
        acc = _rms_norm(acc, fg_ref[...])
    out_ref[0] = acc


def _ffn_block(x, g, w_in, w_out, *, attn=None, final_g=None, tm=512, fc=256):
    bsz, seq, d = x.shape
    d_ff = w_out.shape[0]
    assert d_ff % fc == 0 and fc % V7X_LANES == 0
    win = w_in.astype(BF16)
    tile = pl.BlockSpec((1, tm, d), lambda b, s: (b, s, 0))
    args, specs = [x], [tile]
    est = 2 * 2 * tm * d * 4 + (win.size + w_out.size) * 2 + 6 * tm * d * 4
    if attn is not None:
        o, w_o = attn
        args += [o, w_o.astype(BF16)]
        specs += [pl.BlockSpec((1, tm, o.shape[2]), lambda b, s: (b, s, 0)), _resident(w_o.shape)]
        est += 2 * tm * o.shape[2] * 2 + w_o.size * 2
    args += [g.reshape(1, d), win, w_out.astype(BF16)]
    specs += [_resident((1, d)), _resident(win.shape), _resident(w_out.shape)]
    if final_g is not None:
        args.append(final_g.reshape(1, d))
        specs.append(_resident((1, d)))
    return pl.pallas_call(
        functools.partial(_ffn_kernel, has_attn=attn is not None, final=final_g is not None, fc=fc),
        grid=(bsz, seq // tm),
        in_specs=specs,
        out_specs=tile,
        out_shape=jax.ShapeDtypeStruct((bsz, seq, d), F32),
        compiler_params=pltpu.CompilerParams(
            dimension_semantics=("parallel", "parallel"), vmem_limit_bytes=_vmem_limit(est)),
        name="ffn",
    )(*args)


def _qkv_kernel(x_ref, g_ref, wq_ref, wk_ref, wvt_ref, q_ref, k_ref, vt_ref, *, q_scale):
    ts = x_ref.shape[1]
    hn = _rms_norm(x_ref[0], g_ref[...]).astype(BF16)
    q_ref[0] = (_dot(hn, wq_ref[...]) * q_scale).astype(BF16)
    hp = _to_strided_order(hn)
    k_ref[0] = _dot(hp, wk_ref[...]).astype(BF16)
    vt = _dot_nt(wvt_ref[...], hp).astype(BF16)
    for j in range(ts // KEY_TILE):
        vt_ref[0, j] = vt[:, j * KEY_TILE:(j + 1) * KEY_TILE]


def _qkv_proj(x, g, w_qkv, *, ts=512):
    bsz, seq, d = x.shape
    hd = w_qkv.shape[1] // 3
    wq = w_qkv[:, :hd].astype(BF16)
    wk = w_qkv[:, hd:2 * hd].astype(BF16)
    wvt = w_qkv[:, 2 * hd:].T.astype(BF16)
    q_scale = 0.5 / math.sqrt(HEAD_DIM)
    est = 2 * ts * d * 4 + 3 * d * hd * 2 + 2 * 3 * ts * hd * 2 + 6 * ts * d * 4
    return pl.pallas_call(
        functools.partial(_qkv_kernel, q_scale=q_scale),
        grid=(bsz, seq // ts),
        in_specs=[
            pl.BlockSpec((1, ts, d), lambda b, s: (b, s, 0)),
            _resident((1, d)), _resident((d, hd)), _resident((d, hd)), _resident((hd, d)),
        ],
        out_specs=[
            pl.BlockSpec((1, ts, hd), lambda b, s: (b, s, 0)),
            pl.BlockSpec((1, ts, hd), lambda b, s: (b, s, 0)),
            pl.BlockSpec((1, ts // KEY_TILE, hd, KEY_TILE), lambda b, s: (b, s, 0, 0)),
        ],
        out_shape=[
            jax.ShapeDtypeStruct((bsz, seq, hd), BF16),
            jax.ShapeDtypeStruct((bsz, seq, hd), BF16),
            jax.ShapeDtypeStruct((bsz, seq // KEY_TILE, hd, KEY_TILE), BF16),
        ],
        compiler_params=pltpu.CompilerParams(
            dimension_semantics=("parallel", "parallel"), vmem_limit_bytes=_vmem_limit(est)),
        name="qkv",
    )(x, g.reshape(1, d), wq, wk, wvt)


def _shift_up(x, k):
    sub = lax.broadcasted_iota(jnp.int32, x.shape, 0)
    return jnp.where(sub < V7X_SUBLANES - k, pltpu.roll(x, V7X_SUBLANES - k, 0), 1.0)


def _group_weights(s_g, c, valid):
    shape3 = (KEY_VREGS, V7X_SUBLANES, V7X_LANES)
    beta = (0.5 + 0.5 * jnp.tanh(s_g)).reshape(shape3)
    if valid is not None:
        beta = jnp.where(valid, beta, 0.0)
    run = jnp.ones(shape3[1:], F32)
    part = [None] * KEY_VREGS
    for v in range(KEY_VREGS - 1, -1, -1):
        part[v] = beta[v] * run
        run = run - part[v]
    inc = run
    for k in (1, 2, 4):
        inc = inc * _shift_up(inc, k)
    later = _shift_up(inc, 1) * c
    c = jnp.broadcast_to(inc[0:1, :], inc.shape) * c
    wgt = jnp.concatenate([part[v] * later for v in range(KEY_VREGS)], axis=0)
    return wgt.astype(BF16), c


def _attn_band(kt0, n_tiles, n_diag, skew, *, tri, q_cat, k_ref, vt_ref, carry):
    n_groups = len(carry[0][1])
    nq = n_groups * V7X_LANES
    scores = {}
    for p in range(n_tiles // SCORE_TILES - 1, -1, -1):
        row0 = pl.multiple_of((kt0 + p * SCORE_TILES) * KEY_TILE, KEY_TILE)
        rows = k_ref[0, pl.ds(row0, SCORE_TILES * KEY_TILE), :]
        for lb, q_lb in enumerate(q_cat):
            scores[p, lb] = _dot_nt(rows[:, lb * V7X_LANES:(lb + 1) * V7X_LANES], q_lb)

    carry = [(acc, list(cs)) for acc, cs in carry]
    for d in range(n_tiles - 1, -1, -1):
        p, r = divmod(d, SCORE_TILES)
        j = d - (n_tiles - n_diag)
        for hi, (acc, cs) in enumerate(carry):
            lb, hh = divmod(hi, HEADS_PER_BLOCK)
            w_groups = []
            for g in range(n_groups):
                if g < j or (skew and d < g):
                    w_groups.append(jnp.zeros((KEY_TILE, V7X_LANES), BF16))
                    continue
                col0 = hh * nq + g * V7X_LANES
                s_g = scores[p, lb][r * KEY_TILE:(r + 1) * KEY_TILE, col0:col0 + V7X_LANES]
                w_g, cs[g] = _group_weights(s_g, cs[g], tri if g == j else None)
                w_groups.append(w_g)
            v_blk = vt_ref[0, kt0 + d, hi * HEAD_DIM:(hi + 1) * HEAD_DIM, :]
            carry[hi] = (acc + _dot(v_blk, jnp.concatenate(w_groups, axis=1)), cs)
    return tuple((acc, tuple(cs)) for acc, cs in carry)


def _attn_tail(kt, *, q_cat, k_ref, vt_ref, carry):
    n_groups = len(carry[0][1])
    nq = n_groups * V7X_LANES
    vreg = lax.broadcasted_iota(jnp.int32, (KEY_VREGS, V7X_SUBLANES, V7X_LANES), 0)
    blank = jnp.zeros((KEY_TILE, V7X_LANES), BF16)
    carry = [(acc, list(cs)) for acc, cs in carry]
    for g in range(n_groups):
        tile = jnp.maximum(kt + g, 0)
        on = vreg >= jnp.where(kt + g >= 0, 0, KEY_VREGS)
        rows = k_ref[0, pl.ds(pl.multiple_of(tile * KEY_TILE, KEY_TILE), KEY_TILE), :]
        for lb, q_lb in enumerate(q_cat):
            s_t = _dot_nt(rows[:, lb * V7X_LANES:(lb + 1) * V7X_LANES], q_lb)
            for hh in range(HEADS_PER_BLOCK):
                hi = lb * HEADS_PER_BLOCK + hh
                acc, cs = carry[hi]
                col0 = hh * nq + g * V7X_LANES
                w_g, cs[g] = _group_weights(s_t[:, col0:col0 + V7X_LANES], cs[g], on)
                wgt = jnp.concatenate([w_g if gg == g else blank for gg in range(n_groups)], axis=1)
                v_blk = vt_ref[0, tile, hi * HEAD_DIM:(hi + 1) * HEAD_DIM, :]
                carry[hi] = (acc + _dot(v_blk, wgt), cs)
    return tuple((acc, tuple(cs)) for acc, cs in carry)


def _attn_kernel(q_ref, k_ref, vt_ref, o_ref):
    nq = q_ref.shape[1]
    n_blocks = q_ref.shape[2] // V7X_LANES
    qi = pl.program_id(2)
    lane = lax.broadcasted_iota(jnp.int32, (nq, V7X_LANES), 1)
    q_cat = []
    for lb in range(n_blocks):
        q = q_ref[0, :, lb * V7X_LANES:(lb + 1) * V7X_LANES]
        q_cat.append(jnp.concatenate(
            [jnp.where(lane // HEAD_DIM == hh, q, jnp.zeros_like(q)) for hh in range(HEADS_PER_BLOCK)], axis=0))
    carry = tuple((jnp.zeros((HEAD_DIM, nq), F32),
                   tuple(jnp.ones((V7X_SUBLANES, V7X_LANES), F32) for _ in range(nq // V7X_LANES)))
                  for _ in range(n_blocks * HEADS_PER_BLOCK))
    shape3 = (KEY_VREGS, V7X_SUBLANES, V7X_LANES)
    tri = (lax.broadcasted_iota(jnp.int32, shape3, 1) * KEY_VREGS + lax.broadcasted_iota(jnp.int32, shape3, 0)
           < lax.broadcasted_iota(jnp.int32, shape3, 2))
    n_diag = nq // KEY_TILE
    diag0 = qi * n_diag

    def with_alive(cr):
        c_max = functools.reduce(jnp.maximum, [c for _, cs in cr for c in cs])
        return (jnp.max(c_max) > 0.0).astype(jnp.int32), cr

    band = functools.partial(_attn_band, tri=tri, q_cat=q_cat, k_ref=k_ref, vt_ref=vt_ref, carry=carry)
    live, carry = lax.cond(
        qi == 0,
        lambda: with_alive(band(0, n_diag, n_diag, False)),
        lambda: with_alive(band(diag0 - BAND_BELOW, BAND_BELOW + n_diag, n_diag, True)))

    def more(state):
        kt, live, _ = state
        return jnp.logical_and(kt > -n_diag, live > 0)

    def step(state):
        kt, _, cr = state
        return (kt - 1,) + with_alive(_attn_tail(kt, q_cat=q_cat, k_ref=k_ref, vt_ref=vt_ref, carry=cr))

    kt_first = jnp.where(qi == 0, -n_diag, diag0 - BAND_BELOW - 1)
    _, _, carry = lax.while_loop(more, step, (kt_first, live, carry))

    o_t = jnp.concatenate([acc for acc, _ in carry], axis=0)
    o_ref[0] = o_t.T.astype(BF16)


def _attention(q, k, vt):
    bsz, seq, hd = q.shape
    n_kt = seq // KEY_TILE
    width = ATTN_LANE_BLOCKS * V7X_LANES
    assert KEY_TILE == V7X_LANES and hd % width == 0
    assert BAND_BELOW % SCORE_TILES == 0 and (Q_TILE // KEY_TILE) % SCORE_TILES == 0
    est = 2 * (2 * Q_TILE * width * 2 + seq * width * 2 * 2) + 16 * ATTN_LANE_BLOCKS * KEY_TILE * Q_TILE * 4
    return pl.pallas_call(
        _attn_kernel,
        grid=(bsz, hd // width, seq // Q_TILE),
        in_specs=[
            pl.BlockSpec((1, Q_TILE, width), lambda b, h, i: (b, i, h)),
            pl.BlockSpec((1, seq, width), lambda b, h, i: (b, 0, h)),
            pl.BlockSpec((1, n_kt, width, KEY_TILE), lambda b, h, i: (b, 0, h, 0)),
        ],
        out_specs=pl.BlockSpec((1, Q_TILE, width), lambda b, h, i: (b, i, h)),
        out_shape=jax.ShapeDtypeStruct((bsz, seq, hd), BF16),
        compiler_params=pltpu.CompilerParams(
            dimension_semantics=("parallel", "parallel", "arbitrary"), vmem_limit_bytes=_vmem_limit(est)),
        name="attn",
    )(q, k, vt)


def kernel(x, mix_norm, ffn_norm, final_norm, lru_w_in, lru_b_in, lru_conv_w, lru_conv_b, lru_gate_w, lru_gate_b,
           lru_lambda, lru_w_out, lru_b_out, attn_w_qkv, attn_w_o, ffn_w_in, ffn_w_out):
    depth = mix_norm.shape[0]
    assert x.shape[1] % 512 == 0 and x.shape[2] % V7X_LANES == 0
    h = x
    for layer in range(depth):
        j = layer // 2
        final_g = final_norm if layer == depth - 1 else None
        if layer % 2 == 0:
            h = _lru_block(h, mix_norm[layer], lru_w_in[j], lru_b_in[j], lru_conv_w[j], lru_conv_b[j],
                           lru_gate_w[j], lru_gate_b[j], lru_lambda[j], lru_w_out[j], lru_b_out[j])
            attn = None
        else:
            q, k, vt = _qkv_proj(h, mix_norm[layer], attn_w_qkv[j])
            attn = (_attention(q, k, vt), attn_w_o[j])
        h = _ffn_block(h, ffn_norm[layer], ffn_w_in[layer], ffn_w_out[layer], attn=attn, final_g=final_g)
    return h
```

```python
import functools
import math

import jax
import jax.numpy as jnp
from jax import lax
from jax.experimental import pallas as pl
from jax.experimental.pallas import tpu as pltpu

RMS_EPS = 1e-6
LRU_C = 8.0
HEAD_DIM = 64
N_LRU_BLOCKS = 4

V7X_LANES = 128
V7X_SUBLANES = 8
V7X_VMEM_BYTES = 64 * 1024 * 1024

ORDER_TILE = 128
ORDER_VREGS = ORDER_TILE // V7X_SUBLANES
LRU_STAGE_ROWS = 256
KEY_TILE = ORDER_TILE
KEY_VREGS = ORDER_VREGS
Q_TILE = 256
HEADS_PER_BLOCK = V7X_LANES // HEAD_DIM
ATTN_LANE_BLOCKS = 4
SCORE_TILES = 2
BAND_BELOW = 2

BF16 = jnp.bfloat16
F32 = jnp.float32


def _vmem_limit(n_bytes):
    return int(min(n_bytes * 3 // 2 + (4 << 20), V7X_VMEM_BYTES - (4 << 20)))


def _rms_norm(x, g):
    return x * lax.rsqrt(jnp.mean(x * x, axis=-1, keepdims=True) + RMS_EPS) * g


def _dot(a, b):
    return jnp.dot(a, b, preferred_element_type=F32)


def _dot_nt(a, b):
    return lax.dot_general(a, b, (((1,), (1,)), ((), ())), preferred_element_type=F32)


def _dot_tn(a, b):
    return lax.dot_general(a, b, (((0,), (0,)), ((), ())), preferred_element_type=F32)


def _resident(shape):
    zeros = (0,) * len(shape)
    return pl.BlockSpec(shape, lambda *_: zeros, pipeline_mode=pl.Buffered(1))


def _item_of_row(p):
    return (p % V7X_SUBLANES) * ORDER_VREGS + p // V7X_SUBLANES


def _strided_perm(inverse=False):
    r = lax.broadcasted_iota(jnp.int32, (ORDER_TILE, ORDER_TILE), 0)
    c = lax.broadcasted_iota(jnp.int32, (ORDER_TILE, ORDER_TILE), 1)
    return ((r == _item_of_row(c)) if inverse else (c == _item_of_row(r))).astype(BF16)


def _to_strided_order(x16):
    perm = _strided_perm()
    tiles = [_dot(perm, x16[j * ORDER_TILE:(j + 1) * ORDER_TILE, :]) for j in range(x16.shape[0] // ORDER_TILE)]
    return jnp.concatenate(tiles, axis=0).astype(BF16)


def _shift_down(x, k, fill):
    sub = lax.broadcasted_iota(jnp.int32, x.shape, 0)
    return jnp.where(sub >= k, pltpu.roll(x, k, 0), fill)


def _vregs(x):
    return [x[v * V7X_SUBLANES:(v + 1) * V7X_SUBLANES, :] for v in range(ORDER_VREGS)]


def _lru_kernel(x_ref, g_ref, win_ref, bin_ref, cw_ref, cb_ref, gw_ref, gb_ref, lam_ref, wout_ref, bout_ref,
                o_ref, rec_tail, h_carry, *, conv_width):
    ts = x_ref.shape[1]
    w = wout_ref.shape[0]
    blk = w // N_LRU_BLOCKS
    n_stage = ts // LRU_STAGE_ROWS
    taps = conv_width - 1
    row_shape = (V7X_SUBLANES, w)

    @pl.when(pl.program_id(1) == 0)
    def _():
        rec_tail[...] = jnp.zeros_like(rec_tail)
        h_carry[...] = jnp.zeros_like(h_carry)

    perm, perm_back = _strided_perm(), _strided_perm(inverse=True)
    cw = [jnp.broadcast_to(cw_ref[k:k + 1, :], row_shape) for k in range(conv_width)]
    cb = jnp.broadcast_to(cb_ref[...], row_shape)
    sub = lax.broadcasted_iota(jnp.int32, row_shape, 0)
    nlam = -lam_ref[...]
    log_a_scale = -LRU_C * (jnp.maximum(nlam, 0.0) + jnp.log1p(jnp.exp(-jnp.abs(nlam))))

    subtiles = range(LRU_STAGE_ROWS // ORDER_TILE)

    def subtile(x, j):
        return x[j * ORDER_TILE:(j + 1) * ORDER_TILE, :]

    def permuted(p, x16):
        return jnp.concatenate([_dot(p, subtile(x16, j)) for j in subtiles], axis=0).astype(BF16)

    def normed(t):
        x = x_ref[0, t * LRU_STAGE_ROWS:(t + 1) * LRU_STAGE_ROWS, :]
        return permuted(perm, _rms_norm(x, g_ref[...]).astype(BF16))

    hn = [normed(t) for t in range(n_stage)]

    def in_proj(t):
        u = _dot(hn[t], win_ref[...]) + bin_ref[...]
        return u[:, :w], u[:, w:]

    def conv_gates(rec, last_rows):
        conv_rows = []
        for j in subtiles:
            r = _vregs(subtile(rec, j))
            late = [jnp.where(sub >= 1, pltpu.roll(r[ORDER_VREGS - taps + t], 1, 0), last_rows[t])
                    for t in range(taps)]
            for v in range(ORDER_VREGS):
                acc = cb
                for k in range(conv_width):
                    back = taps - k
                    acc = acc + cw[k] * (r[v - back] if v >= back else late[v - back + taps])
                conv_rows.append(acc)
            last_rows = [r[ORDER_VREGS - taps + t][V7X_SUBLANES - 1:, :] for t in range(taps)]
        conv = jnp.concatenate(conv_rows, axis=0)
        conv16 = conv.astype(BF16)
        gl = [_dot(conv16[:, n * blk:(n + 1) * blk], gw_ref[n]) for n in range(N_LRU_BLOCKS)]
        return conv, gl, last_rows

    def scan(gate_branch, conv, gl, h_in):
        gl_r = jnp.concatenate([g[:, :blk] for g in gl], axis=1) + gb_ref[0:1, :]
        gl_i = jnp.concatenate([g[:, blk:] for g in gl], axis=1) + gb_ref[1:2, :]
        a = jnp.exp(log_a_scale * jax.nn.sigmoid(gl_r))
        one_m = 1.0 - a * a
        mult = jnp.where(one_m > 0.0, one_m * lax.rsqrt(one_m), 0.0)
        b = mult * (jax.nn.sigmoid(gl_i) * conv)
        h_rows = []
        for j in subtiles:
            av, bv = _vregs(subtile(a, j)), _vregs(subtile(b, j))
            h_loc, a_run = [bv[0]], [av[0]]
            for v in range(1, ORDER_VREGS):
                h_loc.append(av[v] * h_loc[-1] + bv[v])
                a_run.append(a_run[-1] * av[v])
            a_cmp, b_cmp = a_run[-1], h_loc[-1]
            for k in (1, 2, 4):
                b_cmp = a_cmp * _shift_down(b_cmp, k, 0.0) + b_cmp
                a_cmp = a_cmp * _shift_down(a_cmp, k, 1.0)
            run_end = a_cmp * h_in + b_cmp
            run_start = _shift_down(run_end, 1, h_in)
            h_rows += [h_loc[v] + a_run[v] * run_start for v in range(ORDER_VREGS)]
            h_in = jnp.broadcast_to(run_end[V7X_SUBLANES - 1:, :], row_shape)
        y = (jax.nn.gelu(gate_branch, approximate=True) * jnp.concatenate(h_rows, axis=0)).astype(BF16)
        return y, h_in

    def out_proj(t, y):
        rows = slice(t * LRU_STAGE_ROWS, (t + 1) * LRU_STAGE_ROWS)
        o_ref[0, rows, :] = _dot(permuted(perm_back, y), wout_ref[...]) + bout_ref[...] + x_ref[0, rows, :]

    last_rows = [rec_tail[t:t + 1, :] for t in range(taps)]
    h_in = h_carry[...]
    projected, gated, scanned = {}, {}, {}
    for step in range(n_stage + 3):
        if step < n_stage:
            projected[step] = in_proj(step)
        if 0 <= step - 1 < n_stage:
            conv, gl, last_rows = conv_gates(projected[step - 1][1], last_rows)
            gated[step - 1] = (conv, gl)
        if 0 <= step - 3 < n_stage:
            out_proj(step - 3, scanned.pop(step - 3))
        if 0 <= step - 2 < n_stage:
            scanned[step - 2], h_in = scan(projected.pop(step - 2)[0], *gated.pop(step - 2), h_in)
    for t in range(taps):
        rec_tail[t:t + 1, :] = last_rows[t]
    h_carry[...] = h_in


def _lru_block(x, g, w_in, b_in, conv_w, conv_b, gate_w, gate_b, lam, w_out, b_out, *, ts=1024):
    bsz, seq, d = x.shape
    assert seq % ts == 0 and ts % LRU_STAGE_ROWS == 0 and LRU_STAGE_ROWS % ORDER_TILE == 0
    w = w_out.shape[0]
    conv_width = conv_w.shape[0]
    blk = w // N_LRU_BLOCKS
    gw = jnp.concatenate([gate_w[0], gate_w[1]], axis=-1).astype(BF16)
    row = lambda v: v.reshape(1, -1)
    est = (2 * 2 * ts * d * 4 + w_in.size * 2 + gw.size * 2 + w_out.size * 2
           + 8 * ts * w * 4)
    return pl.pallas_call(
        functools.partial(_lru_kernel, conv_width=conv_width),
        grid=(bsz, seq // ts),
        in_specs=[
            pl.BlockSpec((1, ts, d), lambda b, s: (b, s, 0)),
            _resident((1, d)),
            _resident((d, 2 * w)),
            _resident((1, 2 * w)),
            _resident((conv_width, w)),
            _resident((1, w)),
            _resident((N_LRU_BLOCKS, blk, 2 * blk)),
            _resident((2, w)),
            _resident((1, w)),
            _resident((w, d)),
            _resident((1, d)),
        ],
        out_specs=pl.BlockSpec((1, ts, d), lambda b, s: (b, s, 0)),
        out_shape=jax.ShapeDtypeStruct((bsz, seq, d), F32),
        scratch_shapes=[pltpu.VMEM((V7X_SUBLANES, w), F32), pltpu.VMEM((V7X_SUBLANES, w), F32)],
        compiler_params=pltpu.CompilerParams(
            dimension_semantics=("parallel", "arbitrary"), vmem_limit_bytes=_vmem_limit(est)),
        name="lru",
    )(x, row(g), w_in.astype(BF16), row(b_in), conv_w, row(conv_b), gw, gate_b, row(lam),
      w_out.astype(BF16), row(b_out))


def _ffn_kernel(*refs, has_attn, final, fc):
    refs = list(refs)
    x_ref = refs.pop(0)
    o_ref, wo_ref = (refs.pop(0), refs.pop(0)) if has_attn else (None, None)
    g_ref, win_ref, wout_ref = refs.pop(0), refs.pop(0), refs.pop(0)
    fg_ref = refs.pop(0) if final else None
    out_ref = refs.pop(0)

    h = x_ref[0]
    if has_attn:
        h = h + _dot_tn(o_ref[0], wo_ref[...])
    hn = _rms_norm(h, g_ref[...]).astype(BF16)
    acc = h
    d_ff = wout_ref.shape[0]

    def gate_up(c):
        return (_dot(hn, win_ref[:, c * fc:(c + 1) * fc]),
                _dot(hn, win_ref[:, d_ff + c * fc:d_ff + (c + 1) * fc]))

    nxt = gate_up(0)
    for c in range(d_ff // fc):
        gate, up = nxt
        if (c + 1) * fc < d_ff:
            nxt = gate_up(c + 1)
        act = (gate * jax.nn.sigmoid(gate) * up).astype(BF16)
        acc = acc + _dot(act, wout_ref[c * fc:(c + 1) * fc, :])
    if final:
        acc = _rms_norm(acc, fg_ref[...])
    out_ref[0] = acc


def _ffn_block(x, g, w_in, w_out, *, attn=None, final_g=None, tm=1024, fc=256):
    bsz, seq, d = x.shape
    d_ff = w_out.shape[0]
    assert d_ff % fc == 0 and fc % V7X_LANES == 0
    win = w_in.astype(BF16)
    tile = pl.BlockSpec((1, tm, d), lambda b, s: (b, s, 0))
    args, specs = [x], [tile]
    est = 2 * 2 * tm * d * 4 + (win.size + w_out.size) * 2 + 6 * tm * d * 4
    if attn is not None:
        o, w_o = attn
        args += [o, w_o.astype(BF16)]
        specs += [pl.BlockSpec((1, o.shape[1], tm), lambda b, s: (b, 0, s)), _resident(w_o.shape)]
        est += 2 * tm * o.shape[1] * 2 + w_o.size * 2
    args += [g.reshape(1, d), win, w_out.astype(BF16)]
    specs += [_resident((1, d)), _resident(win.shape), _resident(w_out.shape)]
    if final_g is not None:
        args.append(final_g.reshape(1, d))
        specs.append(_resident((1, d)))
    return pl.pallas_call(
        functools.partial(_ffn_kernel, has_attn=attn is not None, final=final_g is not None, fc=fc),
        grid=(bsz, seq // tm),
        in_specs=specs,
        out_specs=tile,
        out_shape=jax.ShapeDtypeStruct((bsz, seq, d), F32),
        compiler_params=pltpu.CompilerParams(
            dimension_semantics=("parallel", "parallel"), vmem_limit_bytes=_vmem_limit(est)),
        name="ffn",
    )(*args)


def _qkv_kernel(x_ref, g_ref, wq_ref, wk_ref, wvt_ref, q_ref, k_ref, vt_ref, *, q_scale):
    ts = x_ref.shape[1]
    hn = _rms_norm(x_ref[0], g_ref[...]).astype(BF16)
    q_ref[0] = (_dot(hn, wq_ref[...]) * q_scale).astype(BF16)
    hp = _to_strided_order(hn)
    k_ref[0] = _dot(hp, wk_ref[...]).astype(BF16)
    vt = _dot_nt(wvt_ref[...], hp).astype(BF16)
    for j in range(ts // KEY_TILE):
        vt_ref[0, j] = vt[:, j * KEY_TILE:(j + 1) * KEY_TILE]


def _qkv_proj(x, g, w_qkv, *, ts=512):
    bsz, seq, d = x.shape
    hd = w_qkv.shape[1] // 3
    wq = w_qkv[:, :hd].astype(BF16)
    wk = w_qkv[:, hd:2 * hd].astype(BF16)
    wvt = w_qkv[:, 2 * hd:].T.astype(BF16)
    q_scale = 0.5 / math.sqrt(HEAD_DIM)
    est = 2 * ts * d * 4 + 3 * d * hd * 2 + 2 * 3 * ts * hd * 2 + 6 * ts * d * 4
    return pl.pallas_call(
        functools.partial(_qkv_kernel, q_scale=q_scale),
        grid=(bsz, seq // ts),
        in_specs=[
            pl.BlockSpec((1, ts, d), lambda b, s: (b, s, 0)),
            _resident((1, d)), _resident((d, hd)), _resident((d, hd)), _resident((hd, d)),
        ],
        out_specs=[
            pl.BlockSpec((1, ts, hd), lambda b, s: (b, s, 0)),
            pl.BlockSpec((1, ts, hd), lambda b, s: (b, s, 0)),
            pl.BlockSpec((1, ts // KEY_TILE, hd, KEY_TILE), lambda b, s: (b, s, 0, 0)),
        ],
        out_shape=[
            jax.ShapeDtypeStruct((bsz, seq, hd), BF16),
            jax.ShapeDtypeStruct((bsz, seq, hd), BF16),
            jax.ShapeDtypeStruct((bsz, seq // KEY_TILE, hd, KEY_TILE), BF16),
        ],
        compiler_params=pltpu.CompilerParams(
            dimension_semantics=("parallel", "parallel"), vmem_limit_bytes=_vmem_limit(est)),
        name="qkv",
    )(x, g.reshape(1, d), wq, wk, wvt)


def _shift_up(x, k):
    sub = lax.broadcasted_iota(jnp.int32, x.shape, 0)
    return jnp.where(sub < V7X_SUBLANES - k, pltpu.roll(x, V7X_SUBLANES - k, 0), 1.0)


def _group_weights(s_g, c, valid):
    shape3 = (KEY_VREGS, V7X_SUBLANES, V7X_LANES)
    beta = (0.5 + 0.5 * jnp.tanh(s_g)).reshape(shape3)
    if valid is not None:
        beta = jnp.where(valid, beta, 0.0)
    run = jnp.ones(shape3[1:], F32)
    part = [None] * KEY_VREGS
    for v in range(KEY_VREGS - 1, -1, -1):
        part[v] = beta[v] * run
        run = run - part[v]
    inc = run
    for k in (1, 2, 4):
        inc = inc * _shift_up(inc, k)
    later = _shift_up(inc, 1) * c
    c = jnp.broadcast_to(inc[0:1, :], inc.shape) * c
    wgt = jnp.concatenate([part[v] * later for v in range(KEY_VREGS)], axis=0)
    return wgt.astype(BF16), c


def _attn_band(kt0, n_tiles, n_diag, skew, *, tri, q_cat, k_ref, vt_ref, carry):
    n_groups = len(carry[0][1])
    nq = n_groups * V7X_LANES
    scores = {}
    for p in range(n_tiles // SCORE_TILES - 1, -1, -1):
        row0 = pl.multiple_of((kt0 + p * SCORE_TILES) * KEY_TILE, KEY_TILE)
        rows = k_ref[0, pl.ds(row0, SCORE_TILES * KEY_TILE), :]
        for lb, q_lb in enumerate(q_cat):
            scores[p, lb] = _dot_nt(rows[:, lb * V7X_LANES:(lb + 1) * V7X_LANES], q_lb)

    carry = [(acc, list(cs)) for acc, cs in carry]
    for d in range(n_tiles - 1, -1, -1):
        p, r = divmod(d, SCORE_TILES)
        j = d - (n_tiles - n_diag)
        for hi, (acc, cs) in enumerate(carry):
            lb, hh = divmod(hi, HEADS_PER_BLOCK)
            w_groups = []
            for g in range(n_groups):
                if g < j or (skew and d < g):
                    w_groups.append(jnp.zeros((KEY_TILE, V7X_LANES), BF16))
                    continue
                col0 = hh * nq + g * V7X_LANES
                s_g = scores[p, lb][r * KEY_TILE:(r + 1) * KEY_TILE, col0:col0 + V7X_LANES]
                w_g, cs[g] = _group_weights(s_g, cs[g], tri if g == j else None)
                w_groups.append(w_g)
            v_blk = vt_ref[0, kt0 + d, hi * HEAD_DIM:(hi + 1) * HEAD_DIM, :]
            carry[hi] = (acc + _dot(v_blk, jnp.concatenate(w_groups, axis=1)), cs)
    return tuple((acc, tuple(cs)) for acc, cs in carry)


def _attn_tail(kt, *, q_cat, k_ref, vt_ref, carry):
    n_groups = len(carry[0][1])
    nq = n_groups * V7X_LANES
    vreg = lax.broadcasted_iota(jnp.int32, (KEY_VREGS, V7X_SUBLANES, V7X_LANES), 0)
    blank = jnp.zeros((KEY_TILE, V7X_LANES), BF16)
    carry = [(acc, list(cs)) for acc, cs in carry]
    for g in range(n_groups):
        tile = jnp.maximum(kt + g, 0)
        on = vreg >= jnp.where(kt + g >= 0, 0, KEY_VREGS)
        rows = k_ref[0, pl.ds(pl.multiple_of(tile * KEY_TILE, KEY_TILE), KEY_TILE), :]
        for lb, q_lb in enumerate(q_cat):
            s_t = _dot_nt(rows[:, lb * V7X_LANES:(lb + 1) * V7X_LANES], q_lb)
            for hh in range(HEADS_PER_BLOCK):
                hi = lb * HEADS_PER_BLOCK + hh
                acc, cs = carry[hi]
                col0 = hh * nq + g * V7X_LANES
                w_g, cs[g] = _group_weights(s_t[:, col0:col0 + V7X_LANES], cs[g], on)
                wgt = jnp.concatenate([w_g if gg == g else blank for gg in range(n_groups)], axis=1)
                v_blk = vt_ref[0, tile, hi * HEAD_DIM:(hi + 1) * HEAD_DIM, :]
                carry[hi] = (acc + _dot(v_blk, wgt), cs)
    return tuple((acc, tuple(cs)) for acc, cs in carry)


def _attn_kernel(q_ref, k_ref, vt_ref, o_ref):
    nq = q_ref.shape[1]
    n_blocks = q_ref.shape[2] // V7X_LANES
    qi = pl.program_id(2)
    lane = lax.broadcasted_iota(jnp.int32, (nq, V7X_LANES), 1)
    q_cat = []
    for lb in range(n_blocks):
        q = q_ref[0, :, lb * V7X_LANES:(lb + 1) * V7X_LANES]
        q_cat.append(jnp.concatenate(
            [jnp.where(lane // HEAD_DIM == hh, q, jnp.zeros_like(q)) for hh in range(HEADS_PER_BLOCK)], axis=0))
    carry = tuple((jnp.zeros((HEAD_DIM, nq), F32),
                   tuple(jnp.ones((V7X_SUBLANES, V7X_LANES), F32) for _ in range(nq // V7X_LANES)))
                  for _ in range(n_blocks * HEADS_PER_BLOCK))
    shape3 = (KEY_VREGS, V7X_SUBLANES, V7X_LANES)
    tri = (lax.broadcasted_iota(jnp.int32, shape3, 1) * KEY_VREGS + lax.broadcasted_iota(jnp.int32, shape3, 0)
           < lax.broadcasted_iota(jnp.int32, shape3, 2))
    n_diag = nq // KEY_TILE
    diag0 = qi * n_diag

    def with_alive(cr):
        c_max = functools.reduce(jnp.maximum, [c for _, cs in cr for c in cs])
        return (jnp.max(c_max) > 0.0).astype(jnp.int32), cr

    band = functools.partial(_attn_band, tri=tri, q_cat=q_cat, k_ref=k_ref, vt_ref=vt_ref, carry=carry)
    live, carry = lax.cond(
        qi == 0,
        lambda: with_alive(band(0, n_diag, n_diag, False)),
        lambda: with_alive(band(diag0 - BAND_BELOW, BAND_BELOW + n_diag, n_diag, True)))

    def more(state):
        kt, live, _ = state
        return jnp.logical_and(kt > -n_diag, live > 0)

    def step(state):
        kt, _, cr = state
        return (kt - 1,) + with_alive(_attn_tail(kt, q_cat=q_cat, k_ref=k_ref, vt_ref=vt_ref, carry=cr))

    kt_first = jnp.where(qi == 0, -n_diag, diag0 - BAND_BELOW - 1)
    _, _, carry = lax.while_loop(more, step, (kt_first, live, carry))

    o_ref[0] = jnp.concatenate([acc for acc, _ in carry], axis=0).astype(BF16)


def _attention(q, k, vt):
    bsz, seq, hd = q.shape
    n_kt = seq // KEY_TILE
    width = ATTN_LANE_BLOCKS * V7X_LANES
    assert KEY_TILE == V7X_LANES and hd % width == 0
    assert BAND_BELOW % SCORE_TILES == 0 and (Q_TILE // KEY_TILE) % SCORE_TILES == 0
    est = 2 * (2 * Q_TILE * width * 2 + seq * width * 2 * 2) + 16 * ATTN_LANE_BLOCKS * KEY_TILE * Q_TILE * 4
    return pl.pallas_call(
        _attn_kernel,
        grid=(bsz, hd // width, seq // Q_TILE),
        in_specs=[
            pl.BlockSpec((1, Q_TILE, width), lambda b, h, i: (b, i, h)),
            pl.BlockSpec((1, seq, width), lambda b, h, i: (b, 0, h)),
            pl.BlockSpec((1, n_kt, width, KEY_TILE), lambda b, h, i: (b, 0, h, 0)),
        ],
        out_specs=pl.BlockSpec((1, width, Q_TILE), lambda b, h, i: (b, h, i)),
        out_shape=jax.ShapeDtypeStruct((bsz, hd, seq), BF16),
        compiler_params=pltpu.CompilerParams(
            dimension_semantics=("parallel", "parallel", "arbitrary"), vmem_limit_bytes=_vmem_limit(est)),
        name="attn",
    )(q, k, vt)


def kernel(x, mix_norm, ffn_norm, final_norm, lru_w_in, lru_b_in, lru_conv_w, lru_conv_b, lru_gate_w, lru_gate_b,
           lru_lambda, lru_w_out, lru_b_out, attn_w_qkv, attn_w_o, ffn_w_in, ffn_w_out):
    depth = mix_norm.shape[0]
    assert x.shape[1] % 512 == 0 and x.shape[2] % V7X_LANES == 0
    h = x
    for layer in range(depth):
        j = layer // 2
        final_g = final_norm if layer == depth - 1 else None
        if layer % 2 == 0:
            h = _lru_block(h, mix_norm[layer], lru_w_in[j], lru_b_in[j], lru_conv_w[j], lru_conv_b[j],
                           lru_gate_w[j], lru_gate_b[j], lru_lambda[j], lru_w_out[j], lru_b_out[j])
            attn = None
        else:
            q, k, vt = _qkv_proj(h, mix_norm[layer], attn_w_qkv[j])
            attn = (_attention(q, k, vt), attn_w_o[j])
        h = _ffn_block(h, ffn_norm[layer], ffn_w_in[layer], ffn_w_out[layer], attn=attn, final_g=final_g)
    return h
```

```python
import functools
import math

import jax
import jax.numpy as jnp
from jax import lax
from jax.experimental import pallas as pl
from jax.experimental.pallas import tpu as pltpu

RMS_EPS = 1e-6
LRU_C = 8.0
HEAD_DIM = 64
N_LRU_BLOCKS = 4

V7X_LANES = 128
V7X_SUBLANES = 8
V7X_VMEM_BYTES = 64 * 1024 * 1024

ORDER_TILE = 128
ORDER_VREGS = ORDER_TILE // V7X_SUBLANES
LRU_STAGE_ROWS = 256
KEY_TILE = ORDER_TILE
KEY_VREGS = ORDER_VREGS
Q_TILE = 256
Q_TILES_PER_STEP = 2
HEADS_PER_BLOCK = V7X_LANES // HEAD_DIM
ATTN_LANE_BLOCKS = 4
SCORE_TILES = 2
BAND_BELOW = 2

BF16 = jnp.bfloat16
F32 = jnp.float32


def _vmem_limit(n_bytes):
    return int(min(n_bytes * 3 // 2 + (4 << 20), V7X_VMEM_BYTES - (4 << 20)))


def _rms_norm(x, g):
    return x * lax.rsqrt(jnp.mean(x * x, axis=-1, keepdims=True) + RMS_EPS) * g


def _dot(a, b):
    return jnp.dot(a, b, preferred_element_type=F32)


def _dot_nt(a, b):
    return lax.dot_general(a, b, (((1,), (1,)), ((), ())), preferred_element_type=F32)


def _dot_tn(a, b):
    return lax.dot_general(a, b, (((0,), (0,)), ((), ())), preferred_element_type=F32)


def _resident(shape):
    zeros = (0,) * len(shape)
    return pl.BlockSpec(shape, lambda *_: zeros, pipeline_mode=pl.Buffered(1))


def _item_of_row(p):
    return (p % V7X_SUBLANES) * ORDER_VREGS + p // V7X_SUBLANES


def _strided_perm(inverse=False):
    r = lax.broadcasted_iota(jnp.int32, (ORDER_TILE, ORDER_TILE), 0)
    c = lax.broadcasted_iota(jnp.int32, (ORDER_TILE, ORDER_TILE), 1)
    return ((r == _item_of_row(c)) if inverse else (c == _item_of_row(r))).astype(BF16)


def _to_strided_order(x16):
    perm = _strided_perm()
    tiles = [_dot(perm, x16[j * ORDER_TILE:(j + 1) * ORDER_TILE, :]) for j in range(x16.shape[0] // ORDER_TILE)]
    return jnp.concatenate(tiles, axis=0).astype(BF16)


def _shift_down(x, k, fill):
    sub = lax.broadcasted_iota(jnp.int32, x.shape, 0)
    return jnp.where(sub >= k, pltpu.roll(x, k, 0), fill)


def _vregs(x):
    return [x[v * V7X_SUBLANES:(v + 1) * V7X_SUBLANES, :] for v in range(ORDER_VREGS)]


def _lru_kernel(x_ref, g_ref, win_ref, bin_ref, cw_ref, cb_ref, gw_ref, gb_ref, lam_ref, wout_ref, bout_ref,
                o_ref, rec_tail, h_carry, *, conv_width):
    ts = x_ref.shape[1]
    w = wout_ref.shape[0]
    blk = w // N_LRU_BLOCKS
    n_stage = ts // LRU_STAGE_ROWS
    taps = conv_width - 1
    row_shape = (V7X_SUBLANES, w)

    @pl.when(pl.program_id(1) == 0)
    def _():
        rec_tail[...] = jnp.zeros_like(rec_tail)
        h_carry[...] = jnp.zeros_like(h_carry)

    perm, perm_back = _strided_perm(), _strided_perm(inverse=True)
    cw = [jnp.broadcast_to(cw_ref[k:k + 1, :], row_shape) for k in range(conv_width)]
    cb = jnp.broadcast_to(cb_ref[...], row_shape)
    sub = lax.broadcasted_iota(jnp.int32, row_shape, 0)
    nlam = -lam_ref[...]
    log_a_scale = -LRU_C * (jnp.maximum(nlam, 0.0) + jnp.log1p(jnp.exp(-jnp.abs(nlam))))

    subtiles = range(LRU_STAGE_ROWS // ORDER_TILE)

    def subtile(x, j):
        return x[j * ORDER_TILE:(j + 1) * ORDER_TILE, :]

    def permuted(p, x16):
        return jnp.concatenate([_dot(p, subtile(x16, j)) for j in subtiles], axis=0).astype(BF16)

    def normed(t):
        x = x_ref[0, t * LRU_STAGE_ROWS:(t + 1) * LRU_STAGE_ROWS, :]
        return permuted(perm, _rms_norm(x, g_ref[...]).astype(BF16))

    hn = [normed(t) for t in range(n_stage)]

    def in_proj(t):
        u = _dot(hn[t], win_ref[...]) + bin_ref[...]
        return u[:, :w], u[:, w:]

    def conv_gates(rec, last_rows):
        conv_rows = []
        for j in subtiles:
            r = _vregs(subtile(rec, j))
            late = [jnp.where(sub >= 1, pltpu.roll(r[ORDER_VREGS - taps + t], 1, 0), last_rows[t])
                    for t in range(taps)]
            for v in range(ORDER_VREGS):
                acc = cb
                for k in range(conv_width):
                    back = taps - k
                    acc = acc + cw[k] * (r[v - back] if v >= back else late[v - back + taps])
                conv_rows.append(acc)
            last_rows = [r[ORDER_VREGS - taps + t][V7X_SUBLANES - 1:, :] for t in range(taps)]
        conv = jnp.concatenate(conv_rows, axis=0)
        conv16 = conv.astype(BF16)
        gl = [_dot(conv16[:, n * blk:(n + 1) * blk], gw_ref[n]) for n in range(N_LRU_BLOCKS)]
        return conv, gl, last_rows

    def scan(gate_branch, conv, gl, h_in):
        gl_r = jnp.concatenate([g[:, :blk] for g in gl], axis=1) + gb_ref[0:1, :]
        gl_i = jnp.concatenate([g[:, blk:] for g in gl], axis=1) + gb_ref[1:2, :]
        a = jnp.exp(log_a_scale * jax.nn.sigmoid(gl_r))
        one_m = 1.0 - a * a
        mult = jnp.where(one_m > 0.0, one_m * lax.rsqrt(one_m), 0.0)
        b = mult * (jax.nn.sigmoid(gl_i) * conv)
        h_rows = []
        for j in subtiles:
            av, bv = _vregs(subtile(a, j)), _vregs(subtile(b, j))
            h_loc, a_run = [bv[0]], [av[0]]
            for v in range(1, ORDER_VREGS):
                h_loc.append(av[v] * h_loc[-1] + bv[v])
                a_run.append(a_run[-1] * av[v])
            a_cmp, b_cmp = a_run[-1], h_loc[-1]
            for k in (1, 2, 4):
                b_cmp = a_cmp * _shift_down(b_cmp, k, 0.0) + b_cmp
                a_cmp = a_cmp * _shift_down(a_cmp, k, 1.0)
            run_end = a_cmp * h_in + b_cmp
            run_start = _shift_down(run_end, 1, h_in)
            h_rows += [h_loc[v] + a_run[v] * run_start for v in range(ORDER_VREGS)]
            h_in = jnp.broadcast_to(run_end[V7X_SUBLANES - 1:, :], row_shape)
        y = (jax.nn.gelu(gate_branch, approximate=True) * jnp.concatenate(h_rows, axis=0)).astype(BF16)
        return y, h_in

    def out_proj(t, y):
        rows = slice(t * LRU_STAGE_ROWS, (t + 1) * LRU_STAGE_ROWS)
        o_ref[0, rows, :] = _dot(permuted(perm_back, y), wout_ref[...]) + bout_ref[...] + x_ref[0, rows, :]

    last_rows = [rec_tail[t:t + 1, :] for t in range(taps)]
    h_in = h_carry[...]
    projected, gated, scanned = {}, {}, {}
    for step in range(n_stage + 3):
        if step < n_stage:
            projected[step] = in_proj(step)
        if 0 <= step - 1 < n_stage:
            conv, gl, last_rows = conv_gates(projected[step - 1][1], last_rows)
            gated[step - 1] = (conv, gl)
        if 0 <= step - 3 < n_stage:
            out_proj(step - 3, scanned.pop(step - 3))
        if 0 <= step - 2 < n_stage:
            scanned[step - 2], h_in = scan(projected.pop(step - 2)[0], *gated.pop(step - 2), h_in)
    for t in range(taps):
        rec_tail[t:t + 1, :] = last_rows[t]
    h_carry[...] = h_in


def _lru_block(x, g, w_in, b_in, conv_w, conv_b, gate_w, gate_b, lam, w_out, b_out, *, ts=1024):
    bsz, seq, d = x.shape
    assert seq % ts == 0 and ts % LRU_STAGE_ROWS == 0 and LRU_STAGE_ROWS % ORDER_TILE == 0
    w = w_out.shape[0]
    conv_width = conv_w.shape[0]
    blk = w // N_LRU_BLOCKS
    gw = jnp.concatenate([gate_w[0], gate_w[1]], axis=-1).astype(BF16)
    row = lambda v: v.reshape(1, -1)
    est = (2 * 2 * ts * d * 4 + w_in.size * 2 + gw.size * 2 + w_out.size * 2
           + 8 * ts * w * 4)
    return pl.pallas_call(
        functools.partial(_lru_kernel, conv_width=conv_width),
        grid=(bsz, seq // ts),
        in_specs=[
            pl.BlockSpec((1, ts, d), lambda b, s: (b, s, 0)),
            _resident((1, d)),
            _resident((d, 2 * w)),
            _resident((1, 2 * w)),
            _resident((conv_width, w)),
            _resident((1, w)),
            _resident((N_LRU_BLOCKS, blk, 2 * blk)),
            _resident((2, w)),
            _resident((1, w)),
            _resident((w, d)),
            _resident((1, d)),
        ],
        out_specs=pl.BlockSpec((1, ts, d), lambda b, s: (b, s, 0)),
        out_shape=jax.ShapeDtypeStruct((bsz, seq, d), F32),
        scratch_shapes=[pltpu.VMEM((V7X_SUBLANES, w), F32), pltpu.VMEM((V7X_SUBLANES, w), F32)],
        compiler_params=pltpu.CompilerParams(
            dimension_semantics=("parallel", "arbitrary"), vmem_limit_bytes=_vmem_limit(est)),
        name="lru",
    )(x, row(g), w_in.astype(BF16), row(b_in), conv_w, row(conv_b), gw, gate_b, row(lam),
      w_out.astype(BF16), row(b_out))


def _ffn_kernel(*refs, has_attn, final, fc):
    refs = list(refs)
    x_ref = refs.pop(0)
    o_ref, wo_ref = (refs.pop(0), refs.pop(0)) if has_attn else (None, None)
    g_ref, win_ref, wout_ref = refs.pop(0), refs.pop(0), refs.pop(0)
    fg_ref = refs.pop(0) if final else None
    out_ref = refs.pop(0)

    h = x_ref[0]
    if has_attn:
        h = h + _dot_tn(o_ref[0], wo_ref[...])
    hn = _rms_norm(h, g_ref[...]).astype(BF16)
    acc = h
    d_ff = wout_ref.shape[0]

    def gate_up(c):
        return (_dot(hn, win_ref[:, c * fc:(c + 1) * fc]),
                _dot(hn, win_ref[:, d_ff + c * fc:d_ff + (c + 1) * fc]))

    nxt = gate_up(0)
    for c in range(d_ff // fc):
        gate, up = nxt
        if (c + 1) * fc < d_ff:
            nxt = gate_up(c + 1)
        act = (gate * jax.nn.sigmoid(gate) * up).astype(BF16)
        acc = acc + _dot(act, wout_ref[c * fc:(c + 1) * fc, :])
    if final:
        acc = _rms_norm(acc, fg_ref[...])
    out_ref[0] = acc


def _ffn_block(x, g, w_in, w_out, *, attn=None, final_g=None, tm=1024, fc=256):
    bsz, seq, d = x.shape
    d_ff = w_out.shape[0]
    assert d_ff % fc == 0 and fc % V7X_LANES == 0
    win = w_in.astype(BF16)
    tile = pl.BlockSpec((1, tm, d), lambda b, s: (b, s, 0))
    args, specs = [x], [tile]
    est = 2 * 2 * tm * d * 4 + (win.size + w_out.size) * 2 + 6 * tm * d * 4
    if attn is not None:
        o, w_o = attn
        args += [o, w_o.astype(BF16)]
        specs += [pl.BlockSpec((1, o.shape[1], tm), lambda b, s: (b, 0, s)), _resident(w_o.shape)]
        est += 2 * tm * o.shape[1] * 2 + w_o.size * 2
    args += [g.reshape(1, d), win, w_out.astype(BF16)]
    specs += [_resident((1, d)), _resident(win.shape), _resident(w_out.shape)]
    if final_g is not None:
        args.append(final_g.reshape(1, d))
        specs.append(_resident((1, d)))
    return pl.pallas_call(
        functools.partial(_ffn_kernel, has_attn=attn is not None, final=final_g is not None, fc=fc),
        grid=(bsz, seq // tm),
        in_specs=specs,
        out_specs=tile,
        out_shape=jax.ShapeDtypeStruct((bsz, seq, d), F32),
        compiler_params=pltpu.CompilerParams(
            dimension_semantics=("parallel", "parallel"), vmem_limit_bytes=_vmem_limit(est)),
        name="ffn",
    )(*args)


def _qkv_kernel(x_ref, g_ref, wq_ref, wk_ref, wvt_ref, q_ref, k_ref, vt_ref, *, q_scale):
    ts = x_ref.shape[1]
    hn = _rms_norm(x_ref[0], g_ref[...]).astype(BF16)
    q_ref[0] = (_dot(hn, wq_ref[...]) * q_scale).astype(BF16)
    hp = _to_strided_order(hn)
    k_ref[0] = _dot(hp, wk_ref[...]).astype(BF16)
    vt = _dot_nt(wvt_ref[...], hp).astype(BF16)
    for j in range(ts // KEY_TILE):
        vt_ref[0, j] = vt[:, j * KEY_TILE:(j + 1) * KEY_TILE]


def _qkv_proj(x, g, w_qkv, *, ts=512):
    bsz, seq, d = x.shape
    hd = w_qkv.shape[1] // 3
    wq = w_qkv[:, :hd].astype(BF16)
    wk = w_qkv[:, hd:2 * hd].astype(BF16)
    wvt = w_qkv[:, 2 * hd:].T.astype(BF16)
    q_scale = 0.5 / math.sqrt(HEAD_DIM)
    est = 2 * ts * d * 4 + 3 * d * hd * 2 + 2 * 3 * ts * hd * 2 + 6 * ts * d * 4
    return pl.pallas_call(
        functools.partial(_qkv_kernel, q_scale=q_scale),
        grid=(bsz, seq // ts),
        in_specs=[
            pl.BlockSpec((1, ts, d), lambda b, s: (b, s, 0)),
            _resident((1, d)), _resident((d, hd)), _resident((d, hd)), _resident((hd, d)),
        ],
        out_specs=[
            pl.BlockSpec((1, ts, hd), lambda b, s: (b, s, 0)),
            pl.BlockSpec((1, ts, hd), lambda b, s: (b, s, 0)),
            pl.BlockSpec((1, ts // KEY_TILE, hd, KEY_TILE), lambda b, s: (b, s, 0, 0)),
        ],
        out_shape=[
            jax.ShapeDtypeStruct((bsz, seq, hd), BF16),
            jax.ShapeDtypeStruct((bsz, seq, hd), BF16),
            jax.ShapeDtypeStruct((bsz, seq // KEY_TILE, hd, KEY_TILE), BF16),
        ],
        compiler_params=pltpu.CompilerParams(
            dimension_semantics=("parallel", "parallel"), vmem_limit_bytes=_vmem_limit(est)),
        name="qkv",
    )(x, g.reshape(1, d), wq, wk, wvt)


def _shift_up(x, k):
    sub = lax.broadcasted_iota(jnp.int32, x.shape, 0)
    return jnp.where(sub < V7X_SUBLANES - k, pltpu.roll(x, V7X_SUBLANES - k, 0), 1.0)


def _group_weights(s_g, c, valid):
    shape3 = (KEY_VREGS, V7X_SUBLANES, V7X_LANES)
    beta = (0.5 + 0.5 * jnp.tanh(s_g)).reshape(shape3)
    if valid is not None:
        beta = jnp.where(valid, beta, 0.0)
    run = jnp.ones(shape3[1:], F32)
    part = [None] * KEY_VREGS
    for v in range(KEY_VREGS - 1, -1, -1):
        part[v] = beta[v] * run
        run = run - part[v]
    inc = run
    for k in (1, 2, 4):
        inc = inc * _shift_up(inc, k)
    later = _shift_up(inc, 1) * c
    c = jnp.broadcast_to(inc[0:1, :], inc.shape) * c
    wgt = jnp.concatenate([part[v] * later for v in range(KEY_VREGS)], axis=0)
    return wgt.astype(BF16), c


def _attn_band(kt0, n_tiles, n_diag, skew, *, tri, q_cat, k_ref, vt_ref, carry):
    n_groups = len(carry[0][1])
    nq = n_groups * V7X_LANES
    scores = {}
    for p in range(n_tiles // SCORE_TILES - 1, -1, -1):
        row0 = pl.multiple_of((kt0 + p * SCORE_TILES) * KEY_TILE, KEY_TILE)
        rows = k_ref[0, pl.ds(row0, SCORE_TILES * KEY_TILE), :]
        for lb, q_lb in enumerate(q_cat):
            scores[p, lb] = _dot_nt(rows[:, lb * V7X_LANES:(lb + 1) * V7X_LANES], q_lb)

    carry = [(acc, list(cs)) for acc, cs in carry]
    for d in range(n_tiles - 1, -1, -1):
        p, r = divmod(d, SCORE_TILES)
        j = d - (n_tiles - n_diag)
        for hi, (acc, cs) in enumerate(carry):
            lb, hh = divmod(hi, HEADS_PER_BLOCK)
            w_groups = []
            for g in range(n_groups):
                if g < j or (skew and d < g):
                    w_groups.append(jnp.zeros((KEY_TILE, V7X_LANES), BF16))
                    continue
                col0 = hh * nq + g * V7X_LANES
                s_g = scores[p, lb][r * KEY_TILE:(r + 1) * KEY_TILE, col0:col0 + V7X_LANES]
                w_g, cs[g] = _group_weights(s_g, cs[g], tri if g == j else None)
                w_groups.append(w_g)
            v_blk = vt_ref[0, kt0 + d, hi * HEAD_DIM:(hi + 1) * HEAD_DIM, :]
            carry[hi] = (acc + _dot(v_blk, jnp.concatenate(w_groups, axis=1)), cs)
    return tuple((acc, tuple(cs)) for acc, cs in carry)


def _attn_tail(kt, *, q_cat, k_ref, vt_ref, carry):
    n_groups = len(carry[0][1])
    nq = n_groups * V7X_LANES
    vreg = lax.broadcasted_iota(jnp.int32, (KEY_VREGS, V7X_SUBLANES, V7X_LANES), 0)
    blank = jnp.zeros((KEY_TILE, V7X_LANES), BF16)
    carry = [(acc, list(cs)) for acc, cs in carry]
    for g in range(n_groups):
        tile = jnp.maximum(kt + g, 0)
        on = vreg >= jnp.where(kt + g >= 0, 0, KEY_VREGS)
        rows = k_ref[0, pl.ds(pl.multiple_of(tile * KEY_TILE, KEY_TILE), KEY_TILE), :]
        for lb, q_lb in enumerate(q_cat):
            s_t = _dot_nt(rows[:, lb * V7X_LANES:(lb + 1) * V7X_LANES], q_lb)
            for hh in range(HEADS_PER_BLOCK):
                hi = lb * HEADS_PER_BLOCK + hh
                acc, cs = carry[hi]
                col0 = hh * nq + g * V7X_LANES
                w_g, cs[g] = _group_weights(s_t[:, col0:col0 + V7X_LANES], cs[g], on)
                wgt = jnp.concatenate([w_g if gg == g else blank for gg in range(n_groups)], axis=1)
                v_blk = vt_ref[0, tile, hi * HEAD_DIM:(hi + 1) * HEAD_DIM, :]
                carry[hi] = (acc + _dot(v_blk, wgt), cs)
    return tuple((acc, tuple(cs)) for acc, cs in carry)


def _attn_kernel(q_ref, k_ref, vt_ref, o_ref):
    nq = Q_TILE
    n_blocks = q_ref.shape[2] // V7X_LANES
    n_diag = nq // KEY_TILE
    step_idx = pl.program_id(2)
    shape3 = (KEY_VREGS, V7X_SUBLANES, V7X_LANES)
    tri = (lax.broadcasted_iota(jnp.int32, shape3, 1) * KEY_VREGS + lax.broadcasted_iota(jnp.int32, shape3, 0)
           < lax.broadcasted_iota(jnp.int32, shape3, 2))
    lane = lax.broadcasted_iota(jnp.int32, (nq, V7X_LANES), 1)

    def head_queries(sub):
        q_cat = []
        for lb in range(n_blocks):
            q = q_ref[0, sub * nq:(sub + 1) * nq, lb * V7X_LANES:(lb + 1) * V7X_LANES]
            q_cat.append(jnp.concatenate(
                [jnp.where(lane // HEAD_DIM == hh, q, jnp.zeros_like(q)) for hh in range(HEADS_PER_BLOCK)], axis=0))
        return q_cat

    def diag0(sub):
        return (step_idx * Q_TILES_PER_STEP + sub) * n_diag

    fresh = tuple((jnp.zeros((HEAD_DIM, nq), F32),
                   tuple(jnp.ones((V7X_SUBLANES, V7X_LANES), F32) for _ in range(n_diag)))
                  for _ in range(n_blocks * HEADS_PER_BLOCK))

    def with_alive(cr):
        c_max = functools.reduce(jnp.maximum, [c for _, cs in cr for c in cs])
        return (jnp.max(c_max) > 0.0).astype(jnp.int32), cr

    def bands(at_start):
        out = []
        for sub in range(Q_TILES_PER_STEP):
            common = dict(tri=tri, q_cat=head_queries(sub), k_ref=k_ref, vt_ref=vt_ref, carry=fresh)
            if at_start and sub == 0:
                out.append(with_alive(_attn_band(0, n_diag, n_diag, False, **common)))
            else:
                out.append(with_alive(_attn_band(diag0(sub) - BAND_BELOW, BAND_BELOW + n_diag, n_diag, True,
                                                 **common)))
        return tuple(out)

    results = lax.cond(step_idx == 0, lambda: bands(True), lambda: bands(False))

    for sub, (live, carry) in enumerate(results):
        def more(state):
            kt, live, _ = state
            return jnp.logical_and(kt > -n_diag, live > 0)

        def step(state, sub=sub):
            kt, _, cr = state
            return (kt - 1,) + with_alive(
                _attn_tail(kt, q_cat=head_queries(sub), k_ref=k_ref, vt_ref=vt_ref, carry=cr))

        kt_first = jnp.where(diag0(sub) == 0, -n_diag, diag0(sub) - BAND_BELOW - 1)
        _, _, carry = lax.while_loop(more, step, (kt_first, live, carry))
        o_ref[0, :, sub * nq:(sub + 1) * nq] = jnp.concatenate([acc for acc, _ in carry], axis=0).astype(BF16)


def _attention(q, k, vt):
    bsz, seq, hd = q.shape
    n_kt = seq // KEY_TILE
    width = ATTN_LANE_BLOCKS * V7X_LANES
    assert KEY_TILE == V7X_LANES and hd % width == 0
    assert BAND_BELOW % SCORE_TILES == 0 and (Q_TILE // KEY_TILE) % SCORE_TILES == 0
    q_step = Q_TILE * Q_TILES_PER_STEP
    assert seq % q_step == 0
    est = (2 * (2 * q_step * width * 2 + seq * width * 2 * 2)
           + 16 * Q_TILES_PER_STEP * ATTN_LANE_BLOCKS * KEY_TILE * Q_TILE * 4)
    return pl.pallas_call(
        _attn_kernel,
        grid=(bsz, hd // width, seq // q_step),
        in_specs=[
            pl.BlockSpec((1, q_step, width), lambda b, h, i: (b, i, h)),
            pl.BlockSpec((1, seq, width), lambda b, h, i: (b, 0, h)),
            pl.BlockSpec((1, n_kt, width, KEY_TILE), lambda b, h, i: (b, 0, h, 0)),
        ],
        out_specs=pl.BlockSpec((1, width, q_step), lambda b, h, i: (b, h, i)),
        out_shape=jax.ShapeDtypeStruct((bsz, hd, seq), BF16),
        compiler_params=pltpu.CompilerParams(
            dimension_semantics=("parallel", "parallel", "arbitrary"), vmem_limit_bytes=_vmem_limit(est)),
        name="attn",
    )(q, k, vt)


def kernel(x, mix_norm, ffn_norm, final_norm, lru_w_in, lru_b_in, lru_conv_w, lru_conv_b, lru_gate_w, lru_gate_b,
           lru_lambda, lru_w_out, lru_b_out, attn_w_qkv, attn_w_o, ffn_w_in, ffn_w_out):
    depth = mix_norm.shape[0]
    assert x.shape[1] % 512 == 0 and x.shape[2] % V7X_LANES == 0
    h = x
    for layer in range(depth):
        j = layer // 2
        final_g = final_norm if layer == depth - 1 else None
        if layer % 2 == 0:
            h = _lru_block(h, mix_norm[layer], lru_w_in[j], lru_b_in[j], lru_conv_w[j], lru_conv_b[j],
                           lru_gate_w[j], lru_gate_b[j], lru_lambda[j], lru_w_out[j], lru_b_out[j])
            attn = None
        else:
            q, k, vt = _qkv_proj(h, mix_norm[layer], attn_w_qkv[j])
            attn = (_attention(q, k, vt), attn_w_o[j])
        h = _ffn_block(h, ffn_norm[layer], ffn_w_in[layer], ffn_w_out[layer], attn=attn, final_g=final_g)
    return h
```

```python
import functools
import math

import jax
import jax.numpy as jnp
from jax import lax
from jax.experimental import pallas as pl
from jax.experimental.pallas import tpu as pltpu

RMS_EPS = 1e-6
LRU_C = 8.0
HEAD_DIM = 64
N_LRU_BLOCKS = 4

V7X_LANES = 128
V7X_SUBLANES = 8
V7X_VMEM_BYTES = 64 * 1024 * 1024

ORDER_TILE = 128
ORDER_VREGS = ORDER_TILE // V7X_SUBLANES
LRU_STAGE_ROWS = 256
KEY_TILE = ORDER_TILE
KEY_VREGS = ORDER_VREGS
Q_TILE = 256
Q_TILES_PER_STEP = 4
HEADS_PER_BLOCK = V7X_LANES // HEAD_DIM
ATTN_LANE_BLOCKS = 4
SCORE_TILES = 2
BAND_BELOW = 2

BF16 = jnp.bfloat16
F32 = jnp.float32


def _vmem_limit(n_bytes):
    return int(min(n_bytes * 3 // 2 + (4 << 20), V7X_VMEM_BYTES - (4 << 20)))


def _rms_norm(x, g):
    return x * lax.rsqrt(jnp.mean(x * x, axis=-1, keepdims=True) + RMS_EPS) * g


def _dot(a, b):
    return jnp.dot(a, b, preferred_element_type=F32)


def _dot_nt(a, b):
    return lax.dot_general(a, b, (((1,), (1,)), ((), ())), preferred_element_type=F32)


def _dot_tn(a, b):
    return lax.dot_general(a, b, (((0,), (0,)), ((), ())), preferred_element_type=F32)


def _resident(shape):
    zeros = (0,) * len(shape)
    return pl.BlockSpec(shape, lambda *_: zeros, pipeline_mode=pl.Buffered(1))


def _item_of_row(p):
    return (p % V7X_SUBLANES) * ORDER_VREGS + p // V7X_SUBLANES


def _strided_perm(inverse=False):
    r = lax.broadcasted_iota(jnp.int32, (ORDER_TILE, ORDER_TILE), 0)
    c = lax.broadcasted_iota(jnp.int32, (ORDER_TILE, ORDER_TILE), 1)
    return ((r == _item_of_row(c)) if inverse else (c == _item_of_row(r))).astype(BF16)


def _to_strided_order(x16):
    perm = _strided_perm()
    tiles = [_dot(perm, x16[j * ORDER_TILE:(j + 1) * ORDER_TILE, :]) for j in range(x16.shape[0] // ORDER_TILE)]
    return jnp.concatenate(tiles, axis=0).astype(BF16)


def _shift_down(x, k, fill):
    sub = lax.broadcasted_iota(jnp.int32, x.shape, 0)
    return jnp.where(sub >= k, pltpu.roll(x, k, 0), fill)


def _vregs(x):
    return [x[v * V7X_SUBLANES:(v + 1) * V7X_SUBLANES, :] for v in range(ORDER_VREGS)]


def _lru_kernel(x_ref, g_ref, win_ref, bin_ref, cw_ref, cb_ref, gw_ref, gb_ref, lam_ref, wout_ref, bout_ref,
                o_ref, rec_tail, h_carry, *, conv_width):
    ts = x_ref.shape[1]
    w = wout_ref.shape[0]
    blk = w // N_LRU_BLOCKS
    n_stage = ts // LRU_STAGE_ROWS
    taps = conv_width - 1
    row_shape = (V7X_SUBLANES, w)

    @pl.when(pl.program_id(1) == 0)
    def _():
        rec_tail[...] = jnp.zeros_like(rec_tail)
        h_carry[...] = jnp.zeros_like(h_carry)

    perm, perm_back = _strided_perm(), _strided_perm(inverse=True)
    cw = [jnp.broadcast_to(cw_ref[k:k + 1, :], row_shape) for k in range(conv_width)]
    cb = jnp.broadcast_to(cb_ref[...], row_shape)
    sub = lax.broadcasted_iota(jnp.int32, row_shape, 0)
    nlam = -lam_ref[...]
    log_a_scale = -LRU_C * (jnp.maximum(nlam, 0.0) + jnp.log1p(jnp.exp(-jnp.abs(nlam))))

    subtiles = range(LRU_STAGE_ROWS // ORDER_TILE)

    def subtile(x, j):
        return x[j * ORDER_TILE:(j + 1) * ORDER_TILE, :]

    def permuted(p, x16):
        return jnp.concatenate([_dot(p, subtile(x16, j)) for j in subtiles], axis=0).astype(BF16)

    def normed(t):
        x = x_ref[0, t * LRU_STAGE_ROWS:(t + 1) * LRU_STAGE_ROWS, :]
        return permuted(perm, _rms_norm(x, g_ref[...]).astype(BF16))

    hn = [normed(t) for t in range(n_stage)]

    def in_proj(t):
        u = _dot(hn[t], win_ref[...]) + bin_ref[...]
        return u[:, :w], u[:, w:]

    def conv_gates(rec, last_rows):
        conv_rows = []
        for j in subtiles:
            r = _vregs(subtile(rec, j))
            late = [jnp.where(sub >= 1, pltpu.roll(r[ORDER_VREGS - taps + t], 1, 0), last_rows[t])
                    for t in range(taps)]
            for v in range(ORDER_VREGS):
                acc = cb
                for k in range(conv_width):
                    back = taps - k
                    acc = acc + cw[k] * (r[v - back] if v >= back else late[v - back + taps])
                conv_rows.append(acc)
            last_rows = [r[ORDER_VREGS - taps + t][V7X_SUBLANES - 1:, :] for t in range(taps)]
        conv = jnp.concatenate(conv_rows, axis=0)
        conv16 = conv.astype(BF16)
        gl = [_dot(conv16[:, n * blk:(n + 1) * blk], gw_ref[n]) for n in range(N_LRU_BLOCKS)]
        return conv, gl, last_rows

    def scan(gate_branch, conv, gl, h_in):
        gl_r = jnp.concatenate([g[:, :blk] for g in gl], axis=1) + gb_ref[0:1, :]
        gl_i = jnp.concatenate([g[:, blk:] for g in gl], axis=1) + gb_ref[1:2, :]
        a = jnp.exp(log_a_scale * jax.nn.sigmoid(gl_r))
        one_m = 1.0 - a * a
        mult = jnp.where(one_m > 0.0, one_m * lax.rsqrt(one_m), 0.0)
        b = mult * (jax.nn.sigmoid(gl_i) * conv)
        h_rows = []
        for j in subtiles:
            av, bv = _vregs(subtile(a, j)), _vregs(subtile(b, j))
            h_loc, a_run = [bv[0]], [av[0]]
            for v in range(1, ORDER_VREGS):
                h_loc.append(av[v] * h_loc[-1] + bv[v])
                a_run.append(a_run[-1] * av[v])
            a_cmp, b_cmp = a_run[-1], h_loc[-1]
            for k in (1, 2, 4):
                b_cmp = a_cmp * _shift_down(b_cmp, k, 0.0) + b_cmp
                a_cmp = a_cmp * _shift_down(a_cmp, k, 1.0)
            run_end = a_cmp * h_in + b_cmp
            run_start = _shift_down(run_end, 1, h_in)
            h_rows += [h_loc[v] + a_run[v] * run_start for v in range(ORDER_VREGS)]
            h_in = jnp.broadcast_to(run_end[V7X_SUBLANES - 1:, :], row_shape)
        y = (jax.nn.gelu(gate_branch, approximate=True) * jnp.concatenate(h_rows, axis=0)).astype(BF16)
        return y, h_in

    def out_proj(t, y):
        rows = slice(t * LRU_STAGE_ROWS, (t + 1) * LRU_STAGE_ROWS)
        o_ref[0, rows, :] = _dot(permuted(perm_back, y), wout_ref[...]) + bout_ref[...] + x_ref[0, rows, :]

    last_rows = [rec_tail[t:t + 1, :] for t in range(taps)]
    h_in = h_carry[...]
    projected, gated, scanned = {}, {}, {}
    for step in range(n_stage + 3):
        if step < n_stage:
            projected[step] = in_proj(step)
        if 0 <= step - 1 < n_stage:
            conv, gl, last_rows = conv_gates(projected[step - 1][1], last_rows)
            gated[step - 1] = (conv, gl)
        if 0 <= step - 3 < n_stage:
            out_proj(step - 3, scanned.pop(step - 3))
        if 0 <= step - 2 < n_stage:
            scanned[step - 2], h_in = scan(projected.pop(step - 2)[0], *gated.pop(step - 2), h_in)
    for t in range(taps):
        rec_tail[t:t + 1, :] = last_rows[t]
    h_carry[...] = h_in


def _lru_block(x, g, w_in, b_in, conv_w, conv_b, gate_w, gate_b, lam, w_out, b_out, *, ts=1024):
    bsz, seq, d = x.shape
    assert seq % ts == 0 and ts % LRU_STAGE_ROWS == 0 and LRU_STAGE_ROWS % ORDER_TILE == 0
    w = w_out.shape[0]
    conv_width = conv_w.shape[0]
    blk = w // N_LRU_BLOCKS
    gw = jnp.concatenate([gate_w[0], gate_w[1]], axis=-1).astype(BF16)
    row = lambda v: v.reshape(1, -1)
    est = (2 * 2 * ts * d * 4 + w_in.size * 2 + gw.size * 2 + w_out.size * 2
           + 8 * ts * w * 4)
    return pl.pallas_call(
        functools.partial(_lru_kernel, conv_width=conv_width),
        grid=(bsz, seq // ts),
        in_specs=[
            pl.BlockSpec((1, ts, d), lambda b, s: (b, s, 0)),
            _resident((1, d)),
            _resident((d, 2 * w)),
            _resident((1, 2 * w)),
            _resident((conv_width, w)),
            _resident((1, w)),
            _resident((N_LRU_BLOCKS, blk, 2 * blk)),
            _resident((2, w)),
            _resident((1, w)),
            _resident((w, d)),
            _resident((1, d)),
        ],
        out_specs=pl.BlockSpec((1, ts, d), lambda b, s: (b, s, 0)),
        out_shape=jax.ShapeDtypeStruct((bsz, seq, d), F32),
        scratch_shapes=[pltpu.VMEM((V7X_SUBLANES, w), F32), pltpu.VMEM((V7X_SUBLANES, w), F32)],
        compiler_params=pltpu.CompilerParams(
            dimension_semantics=("parallel", "arbitrary"), vmem_limit_bytes=_vmem_limit(est)),
        name="lru",
    )(x, row(g), w_in.astype(BF16), row(b_in), conv_w, row(conv_b), gw, gate_b, row(lam),
      w_out.astype(BF16), row(b_out))


def _ffn_kernel(*refs, has_attn, final, fc):
    refs = list(refs)
    x_ref = refs.pop(0)
    o_ref, wo_ref = (refs.pop(0), refs.pop(0)) if has_attn else (None, None)
    g_ref, win_ref, wout_ref = refs.pop(0), refs.pop(0), refs.pop(0)
    fg_ref = refs.pop(0) if final else None
    out_ref = refs.pop(0)

    h = x_ref[0]
    if has_attn:
        h = h + _dot_tn(o_ref[0], wo_ref[...])
    hn = _rms_norm(h, g_ref[...]).astype(BF16)
    acc = h
    d_ff = wout_ref.shape[0]

    def gate_up(c):
        return (_dot(hn, win_ref[:, c * fc:(c + 1) * fc]),
                _dot(hn, win_ref[:, d_ff + c * fc:d_ff + (c + 1) * fc]))

    nxt = gate_up(0)
    for c in range(d_ff // fc):
        gate, up = nxt
        if (c + 1) * fc < d_ff:
            nxt = gate_up(c + 1)
        act = (gate * jax.nn.sigmoid(gate) * up).astype(BF16)
        acc = acc + _dot(act, wout_ref[c * fc:(c + 1) * fc, :])
    if final:
        acc = _rms_norm(acc, fg_ref[...])
    out_ref[0] = acc


def _ffn_block(x, g, w_in, w_out, *, attn=None, final_g=None, tm=1024, fc=256):
    bsz, seq, d = x.shape
    d_ff = w_out.shape[0]
    assert d_ff % fc == 0 and fc % V7X_LANES == 0
    win = w_in.astype(BF16)
    tile = pl.BlockSpec((1, tm, d), lambda b, s: (b, s, 0))
    args, specs = [x], [tile]
    est = 2 * 2 * tm * d * 4 + (win.size + w_out.size) * 2 + 6 * tm * d * 4
    if attn is not None:
        o, w_o = attn
        args += [o, w_o.astype(BF16)]
        specs += [pl.BlockSpec((1, o.shape[1], tm), lambda b, s: (b, 0, s)), _resident(w_o.shape)]
        est += 2 * tm * o.shape[1] * 2 + w_o.size * 2
    args += [g.reshape(1, d), win, w_out.astype(BF16)]
    specs += [_resident((1, d)), _resident(win.shape), _resident(w_out.shape)]
    if final_g is not None:
        args.append(final_g.reshape(1, d))
        specs.append(_resident((1, d)))
    return pl.pallas_call(
        functools.partial(_ffn_kernel, has_attn=attn is not None, final=final_g is not None, fc=fc),
        grid=(bsz, seq // tm),
        in_specs=specs,
        out_specs=tile,
        out_shape=jax.ShapeDtypeStruct((bsz, seq, d), F32),
        compiler_params=pltpu.CompilerParams(
            dimension_semantics=("parallel", "parallel"), vmem_limit_bytes=_vmem_limit(est)),
        name="ffn",
    )(*args)


def _qkv_kernel(x_ref, g_ref, wq_ref, wk_ref, wvt_ref, q_ref, k_ref, vt_ref, *, q_scale):
    ts = x_ref.shape[1]
    hn = _rms_norm(x_ref[0], g_ref[...]).astype(BF16)
    q_ref[0] = (_dot(hn, wq_ref[...]) * q_scale).astype(BF16)
    hp = _to_strided_order(hn)
    k_ref[0] = _dot(hp, wk_ref[...]).astype(BF16)
    vt = _dot_nt(wvt_ref[...], hp).astype(BF16)
    for j in range(ts // KEY_TILE):
        vt_ref[0, j] = vt[:, j * KEY_TILE:(j + 1) * KEY_TILE]


def _qkv_proj(x, g, w_qkv, *, ts=512):
    bsz, seq, d = x.shape
    hd = w_qkv.shape[1] // 3
    wq = w_qkv[:, :hd].astype(BF16)
    wk = w_qkv[:, hd:2 * hd].astype(BF16)
    wvt = w_qkv[:, 2 * hd:].T.astype(BF16)
    q_scale = 0.5 / math.sqrt(HEAD_DIM)
    est = 2 * ts * d * 4 + 3 * d * hd * 2 + 2 * 3 * ts * hd * 2 + 6 * ts * d * 4
    return pl.pallas_call(
        functools.partial(_qkv_kernel, q_scale=q_scale),
        grid=(bsz, seq // ts),
        in_specs=[
            pl.BlockSpec((1, ts, d), lambda b, s: (b, s, 0)),
            _resident((1, d)), _resident((d, hd)), _resident((d, hd)), _resident((hd, d)),
        ],
        out_specs=[
            pl.BlockSpec((1, ts, hd), lambda b, s: (b, s, 0)),
            pl.BlockSpec((1, ts, hd), lambda b, s: (b, s, 0)),
            pl.BlockSpec((1, ts // KEY_TILE, hd, KEY_TILE), lambda b, s: (b, s, 0, 0)),
        ],
        out_shape=[
            jax.ShapeDtypeStruct((bsz, seq, hd), BF16),
            jax.ShapeDtypeStruct((bsz, seq, hd), BF16),
            jax.ShapeDtypeStruct((bsz, seq // KEY_TILE, hd, KEY_TILE), BF16),
        ],
        compiler_params=pltpu.CompilerParams(
            dimension_semantics=("parallel", "parallel"), vmem_limit_bytes=_vmem_limit(est)),
        name="qkv",
    )(x, g.reshape(1, d), wq, wk, wvt)


def _shift_up(x, k):
    sub = lax.broadcasted_iota(jnp.int32, x.shape, 0)
    return jnp.where(sub < V7X_SUBLANES - k, pltpu.roll(x, V7X_SUBLANES - k, 0), 1.0)


def _group_weights(s_g, c, valid):
    shape3 = (KEY_VREGS, V7X_SUBLANES, V7X_LANES)
    beta = (0.5 + 0.5 * jnp.tanh(s_g)).reshape(shape3)
    if valid is not None:
        beta = jnp.where(valid, beta, 0.0)
    run = jnp.ones(shape3[1:], F32)
    part = [None] * KEY_VREGS
    for v in range(KEY_VREGS - 1, -1, -1):
        part[v] = beta[v] * run
        run = run - part[v]
    inc = run
    for k in (1, 2, 4):
        inc = inc * _shift_up(inc, k)
    later = _shift_up(inc, 1) * c
    c = jnp.broadcast_to(inc[0:1, :], inc.shape) * c
    wgt = jnp.concatenate([part[v] * later for v in range(KEY_VREGS)], axis=0)
    return wgt.astype(BF16), c


def _attn_band(kt0, n_tiles, n_diag, skew, *, tri, q_cat, k_ref, vt_ref, carry):
    n_groups = len(carry[0][1])
    nq = n_groups * V7X_LANES
    scores = {}
    for p in range(n_tiles // SCORE_TILES - 1, -1, -1):
        row0 = pl.multiple_of((kt0 + p * SCORE_TILES) * KEY_TILE, KEY_TILE)
        rows = k_ref[0, pl.ds(row0, SCORE_TILES * KEY_TILE), :]
        for lb, q_lb in enumerate(q_cat):
            scores[p, lb] = _dot_nt(rows[:, lb * V7X_LANES:(lb + 1) * V7X_LANES], q_lb)

    carry = [(acc, list(cs)) for acc, cs in carry]
    for d in range(n_tiles - 1, -1, -1):
        p, r = divmod(d, SCORE_TILES)
        j = d - (n_tiles - n_diag)
        for hi, (acc, cs) in enumerate(carry):
            lb, hh = divmod(hi, HEADS_PER_BLOCK)
            w_groups = []
            for g in range(n_groups):
                if g < j or (skew and d < g):
                    w_groups.append(jnp.zeros((KEY_TILE, V7X_LANES), BF16))
                    continue
                col0 = hh * nq + g * V7X_LANES
                s_g = scores[p, lb][r * KEY_TILE:(r + 1) * KEY_TILE, col0:col0 + V7X_LANES]
                w_g, cs[g] = _group_weights(s_g, cs[g], tri if g == j else None)
                w_groups.append(w_g)
            v_blk = vt_ref[0, kt0 + d, hi * HEAD_DIM:(hi + 1) * HEAD_DIM, :]
            carry[hi] = (acc + _dot(v_blk, jnp.concatenate(w_groups, axis=1)), cs)
    return tuple((acc, tuple(cs)) for acc, cs in carry)


def _attn_tail(kt, *, q_cat, k_ref, vt_ref, carry):
    n_groups = len(carry[0][1])
    nq = n_groups * V7X_LANES
    vreg = lax.broadcasted_iota(jnp.int32, (KEY_VREGS, V7X_SUBLANES, V7X_LANES), 0)
    blank = jnp.zeros((KEY_TILE, V7X_LANES), BF16)
    carry = [(acc, list(cs)) for acc, cs in carry]
    for g in range(n_groups):
        tile = jnp.maximum(kt + g, 0)
        on = vreg >= jnp.where(kt + g >= 0, 0, KEY_VREGS)
        rows = k_ref[0, pl.ds(pl.multiple_of(tile * KEY_TILE, KEY_TILE), KEY_TILE), :]
        for lb, q_lb in enumerate(q_cat):
            s_t = _dot_nt(rows[:, lb * V7X_LANES:(lb + 1) * V7X_LANES], q_lb)
            for hh in range(HEADS_PER_BLOCK):
                hi = lb * HEADS_PER_BLOCK + hh
                acc, cs = carry[hi]
                col0 = hh * nq + g * V7X_LANES
                w_g, cs[g] = _group_weights(s_t[:, col0:col0 + V7X_LANES], cs[g], on)
                wgt = jnp.concatenate([w_g if gg == g else blank for gg in range(n_groups)], axis=1)
                v_blk = vt_ref[0, tile, hi * HEAD_DIM:(hi + 1) * HEAD_DIM, :]
                carry[hi] = (acc + _dot(v_blk, wgt), cs)
    return tuple((acc, tuple(cs)) for acc, cs in carry)


def _attn_kernel(q_ref, k_ref, vt_ref, o_ref):
    nq = Q_TILE
    n_blocks = q_ref.shape[2] // V7X_LANES
    n_diag = nq // KEY_TILE
    step_idx = pl.program_id(2)
    shape3 = (KEY_VREGS, V7X_SUBLANES, V7X_LANES)
    tri = (lax.broadcasted_iota(jnp.int32, shape3, 1) * KEY_VREGS + lax.broadcasted_iota(jnp.int32, shape3, 0)
           < lax.broadcasted_iota(jnp.int32, shape3, 2))
    lane = lax.broadcasted_iota(jnp.int32, (nq, V7X_LANES), 1)

    def head_queries(sub):
        q_cat = []
        for lb in range(n_blocks):
            q = q_ref[0, sub * nq:(sub + 1) * nq, lb * V7X_LANES:(lb + 1) * V7X_LANES]
            q_cat.append(jnp.concatenate(
                [jnp.where(lane // HEAD_DIM == hh, q, jnp.zeros_like(q)) for hh in range(HEADS_PER_BLOCK)], axis=0))
        return q_cat

    def diag0(sub):
        return (step_idx * Q_TILES_PER_STEP + sub) * n_diag

    fresh = tuple((jnp.zeros((HEAD_DIM, nq), F32),
                   tuple(jnp.ones((V7X_SUBLANES, V7X_LANES), F32) for _ in range(n_diag)))
                  for _ in range(n_blocks * HEADS_PER_BLOCK))

    def with_alive(cr):
        c_max = functools.reduce(jnp.maximum, [c for _, cs in cr for c in cs])
        return (jnp.max(c_max) > 0.0).astype(jnp.int32), cr

    def bands(at_start):
        out = []
        for sub in range(Q_TILES_PER_STEP):
            common = dict(tri=tri, q_cat=head_queries(sub), k_ref=k_ref, vt_ref=vt_ref, carry=fresh)
            if at_start and sub == 0:
                out.append(with_alive(_attn_band(0, n_diag, n_diag, False, **common)))
            else:
                out.append(with_alive(_attn_band(diag0(sub) - BAND_BELOW, BAND_BELOW + n_diag, n_diag, True,
                                                 **common)))
        return tuple(out)

    results = lax.cond(step_idx == 0, lambda: bands(True), lambda: bands(False))

    for sub, (live, carry) in enumerate(results):
        def more(state):
            kt, live, _ = state
            return jnp.logical_and(kt > -n_diag, live > 0)

        def step(state, sub=sub):
            kt, _, cr = state
            return (kt - 1,) + with_alive(
                _attn_tail(kt, q_cat=head_queries(sub), k_ref=k_ref, vt_ref=vt_ref, carry=cr))

        kt_first = jnp.where(diag0(sub) == 0, -n_diag, diag0(sub) - BAND_BELOW - 1)
        _, _, carry = lax.while_loop(more, step, (kt_first, live, carry))
        o_ref[0, :, sub * nq:(sub + 1) * nq] = jnp.concatenate([acc for acc, _ in carry], axis=0).astype(BF16)


def _attention(q, k, vt):
    bsz, seq, hd = q.shape
    n_kt = seq // KEY_TILE
    width = ATTN_LANE_BLOCKS * V7X_LANES
    assert KEY_TILE == V7X_LANES and hd % width == 0
    assert BAND_BELOW % SCORE_TILES == 0 and (Q_TILE // KEY_TILE) % SCORE_TILES == 0
    q_step = Q_TILE * Q_TILES_PER_STEP
    assert seq % q_step == 0
    est = (2 * (2 * q_step * width * 2 + seq * width * 2 * 2)
           + 16 * Q_TILES_PER_STEP * ATTN_LANE_BLOCKS * KEY_TILE * Q_TILE * 4)
    return pl.pallas_call(
        _attn_kernel,
        grid=(bsz, hd // width, seq // q_step),
        in_specs=[
            pl.BlockSpec((1, q_step, width), lambda b, h, i: (b, i, h)),
            pl.BlockSpec((1, seq, width), lambda b, h, i: (b, 0, h)),
            pl.BlockSpec((1, n_kt, width, KEY_TILE), lambda b, h, i: (b, 0, h, 0)),
        ],
        out_specs=pl.BlockSpec((1, width, q_step), lambda b, h, i: (b, h, i)),
        out_shape=jax.ShapeDtypeStruct((bsz, hd, seq), BF16),
        compiler_params=pltpu.CompilerParams(
            dimension_semantics=("parallel", "parallel", "arbitrary"), vmem_limit_bytes=_vmem_limit(est)),
        name="attn",
    )(q, k, vt)


def kernel(x, mix_norm, ffn_norm, final_norm, lru_w_in, lru_b_in, lru_conv_w, lru_conv_b, lru_gate_w, lru_gate_b,
           lru_lambda, lru_w_out, lru_b_out, attn_w_qkv, attn_w_o, ffn_w_in, ffn_w_out):
    depth = mix_norm.shape[0]
    assert x.shape[1] % 512 == 0 and x.shape[2] % V7X_LANES == 0
    h = x
    for layer in range(depth):
        j = layer // 2
        final_g = final_norm if layer == depth - 1 else None
        if layer % 2 == 0:
            h = _lru_block(h, mix_norm[layer], lru_w_in[j], lru_b_in[j], lru_conv_w[j], lru_conv_b[j],
                           lru_gate_w[j], lru_gate_b[j], lru_lambda[j], lru_w_out[j], lru_b_out[j])
            attn = None
        else:
            q, k, vt = _qkv_proj(h, mix_norm[layer], attn_w_qkv[j])
            attn = (_attention(q, k, vt), attn_w_o[j])
        h = _ffn_block(h, ffn_norm[layer], ffn_w_in[layer], ffn_w_out[layer], attn=attn, final_g=final_g)
    return h
```

```python
import functools
import math

import jax
import jax.numpy as jnp
from jax import lax
from jax.experimental import pallas as pl
from jax.experimental.pallas import tpu as pltpu

RMS_EPS = 1e-6
LRU_C = 8.0
HEAD_DIM = 64
N_LRU_BLOCKS = 4

V7X_LANES = 128
V7X_SUBLANES = 8
V7X_VMEM_BYTES = 64 * 1024 * 1024

ORDER_TILE = 128
ORDER_VREGS = ORDER_TILE // V7X_SUBLANES
LRU_STAGE_ROWS = 256
KEY_TILE = ORDER_TILE
KEY_VREGS = ORDER_VREGS
Q_TILE = 256
Q_TILES_PER_STEP = 4
HEADS_PER_BLOCK = V7X_LANES // HEAD_DIM
ATTN_LANE_BLOCKS = 4
SCORE_TILES = 2
BAND_BELOW = 2
GROUP_LAG = 16

BF16 = jnp.bfloat16
F32 = jnp.float32


def _vmem_limit(n_bytes):
    return int(min(n_bytes * 3 // 2 + (4 << 20), V7X_VMEM_BYTES - (4 << 20)))


def _rms_norm(x, g):
    return x * lax.rsqrt(jnp.mean(x * x, axis=-1, keepdims=True) + RMS_EPS) * g


def _dot(a, b):
    return jnp.dot(a, b, preferred_element_type=F32)


def _dot_nt(a, b):
    return lax.dot_general(a, b, (((1,), (1,)), ((), ())), preferred_element_type=F32)


def _dot_tn(a, b):
    return lax.dot_general(a, b, (((0,), (0,)), ((), ())), preferred_element_type=F32)


def _resident(shape):
    zeros = (0,) * len(shape)
    return pl.BlockSpec(shape, lambda *_: zeros, pipeline_mode=pl.Buffered(1))


def _item_of_row(p):
    return (p % V7X_SUBLANES) * ORDER_VREGS + p // V7X_SUBLANES


def _strided_perm(inverse=False):
    r = lax.broadcasted_iota(jnp.int32, (ORDER_TILE, ORDER_TILE), 0)
    c = lax.broadcasted_iota(jnp.int32, (ORDER_TILE, ORDER_TILE), 1)
    return ((r == _item_of_row(c)) if inverse else (c == _item_of_row(r))).astype(BF16)


def _to_strided_order(x16):
    perm = _strided_perm()
    tiles = [_dot(perm, x16[j * ORDER_TILE:(j + 1) * ORDER_TILE, :]) for j in range(x16.shape[0] // ORDER_TILE)]
    return jnp.concatenate(tiles, axis=0).astype(BF16)


def _shift_down(x, k, fill):
    sub = lax.broadcasted_iota(jnp.int32, x.shape, 0)
    return jnp.where(sub >= k, pltpu.roll(x, k, 0), fill)


def _vregs(x):
    return [x[v * V7X_SUBLANES:(v + 1) * V7X_SUBLANES, :] for v in range(ORDER_VREGS)]


def _lru_kernel(x_ref, g_ref, win_ref, bin_ref, cw_ref, cb_ref, gw_ref, gb_ref, lam_ref, wout_ref, bout_ref,
                o_ref, rec_tail, h_carry, *, conv_width):
    ts = x_ref.shape[1]
    w = wout_ref.shape[0]
    blk = w // N_LRU_BLOCKS
    n_stage = ts // LRU_STAGE_ROWS
    taps = conv_width - 1
    row_shape = (V7X_SUBLANES, w)

    @pl.when(pl.program_id(1) == 0)
    def _():
        rec_tail[...] = jnp.zeros_like(rec_tail)
        h_carry[...] = jnp.zeros_like(h_carry)

    perm, perm_back = _strided_perm(), _strided_perm(inverse=True)
    cw = [jnp.broadcast_to(cw_ref[k:k + 1, :], row_shape) for k in range(conv_width)]
    cb = jnp.broadcast_to(cb_ref[...], row_shape)
    sub = lax.broadcasted_iota(jnp.int32, row_shape, 0)
    nlam = -lam_ref[...]
    log_a_scale = -LRU_C * (jnp.maximum(nlam, 0.0) + jnp.log1p(jnp.exp(-jnp.abs(nlam))))

    subtiles = range(LRU_STAGE_ROWS // ORDER_TILE)

    def subtile(x, j):
        return x[j * ORDER_TILE:(j + 1) * ORDER_TILE, :]

    def permuted(p, x16):
        return jnp.concatenate([_dot(p, subtile(x16, j)) for j in subtiles], axis=0).astype(BF16)

    def normed(t):
        x = x_ref[0, t * LRU_STAGE_ROWS:(t + 1) * LRU_STAGE_ROWS, :]
        return permuted(perm, _rms_norm(x, g_ref[...]).astype(BF16))

    hn = [normed(t) for t in range(n_stage)]

    def in_proj(t):
        u = _dot(hn[t], win_ref[...]) + bin_ref[...]
        return u[:, :w], u[:, w:]

    def conv_gates(rec, last_rows):
        conv_rows = []
        for j in subtiles:
            r = _vregs(subtile(rec, j))
            late = [jnp.where(sub >= 1, pltpu.roll(r[ORDER_VREGS - taps + t], 1, 0), last_rows[t])
                    for t in range(taps)]
            for v in range(ORDER_VREGS):
                acc = cb
                for k in range(conv_width):
                    back = taps - k
                    acc = acc + cw[k] * (r[v - back] if v >= back else late[v - back + taps])
                conv_rows.append(acc)
            last_rows = [r[ORDER_VREGS - taps + t][V7X_SUBLANES - 1:, :] for t in range(taps)]
        conv = jnp.concatenate(conv_rows, axis=0)
        conv16 = conv.astype(BF16)
        gl = [_dot(conv16[:, n * blk:(n + 1) * blk], gw_ref[n]) for n in range(N_LRU_BLOCKS)]
        return conv, gl, last_rows

    def scan(gate_branch, conv, gl, h_in):
        gl_r = jnp.concatenate([g[:, :blk] for g in gl], axis=1) + gb_ref[0:1, :]
        gl_i = jnp.concatenate([g[:, blk:] for g in gl], axis=1) + gb_ref[1:2, :]
        a = jnp.exp(log_a_scale * jax.nn.sigmoid(gl_r))
        one_m = 1.0 - a * a
        mult = jnp.where(one_m > 0.0, one_m * lax.rsqrt(one_m), 0.0)
        b = mult * (jax.nn.sigmoid(gl_i) * conv)
        h_rows = []
        for j in subtiles:
            av, bv = _vregs(subtile(a, j)), _vregs(subtile(b, j))
            h_loc, a_run = [bv[0]], [av[0]]
            for v in range(1, ORDER_VREGS):
                h_loc.append(av[v] * h_loc[-1] + bv[v])
                a_run.append(a_run[-1] * av[v])
            a_cmp, b_cmp = a_run[-1], h_loc[-1]
            for k in (1, 2, 4):
                b_cmp = a_cmp * _shift_down(b_cmp, k, 0.0) + b_cmp
                a_cmp = a_cmp * _shift_down(a_cmp, k, 1.0)
            run_end = a_cmp * h_in + b_cmp
            run_start = _shift_down(run_end, 1, h_in)
            h_rows += [h_loc[v] + a_run[v] * run_start for v in range(ORDER_VREGS)]
            h_in = jnp.broadcast_to(run_end[V7X_SUBLANES - 1:, :], row_shape)
        y = (jax.nn.gelu(gate_branch, approximate=True) * jnp.concatenate(h_rows, axis=0)).astype(BF16)
        return y, h_in

    def out_proj(t, y):
        rows = slice(t * LRU_STAGE_ROWS, (t + 1) * LRU_STAGE_ROWS)
        o_ref[0, rows, :] = _dot(permuted(perm_back, y), wout_ref[...]) + bout_ref[...] + x_ref[0, rows, :]

    last_rows = [rec_tail[t:t + 1, :] for t in range(taps)]
    h_in = h_carry[...]
    projected, gated, scanned = {}, {}, {}
    for step in range(n_stage + 3):
        if step < n_stage:
            projected[step] = in_proj(step)
        if 0 <= step - 1 < n_stage:
            conv, gl, last_rows = conv_gates(projected[step - 1][1], last_rows)
            gated[step - 1] = (conv, gl)
        if 0 <= step - 3 < n_stage:
            out_proj(step - 3, scanned.pop(step - 3))
        if 0 <= step - 2 < n_stage:
            scanned[step - 2], h_in = scan(projected.pop(step - 2)[0], *gated.pop(step - 2), h_in)
    for t in range(taps):
        rec_tail[t:t + 1, :] = last_rows[t]
    h_carry[...] = h_in


def _lru_block(x, g, w_in, b_in, conv_w, conv_b, gate_w, gate_b, lam, w_out, b_out, *, ts=1024):
    bsz, seq, d = x.shape
    assert seq % ts == 0 and ts % LRU_STAGE_ROWS == 0 and LRU_STAGE_ROWS % ORDER_TILE == 0
    w = w_out.shape[0]
    conv_width = conv_w.shape[0]
    blk = w // N_LRU_BLOCKS
    gw = jnp.concatenate([gate_w[0], gate_w[1]], axis=-1).astype(BF16)
    row = lambda v: v.reshape(1, -1)
    est = (2 * 2 * ts * d * 4 + w_in.size * 2 + gw.size * 2 + w_out.size * 2
           + 8 * ts * w * 4)
    return pl.pallas_call(
        functools.partial(_lru_kernel, conv_width=conv_width),
        grid=(bsz, seq // ts),
        in_specs=[
            pl.BlockSpec((1, ts, d), lambda b, s: (b, s, 0)),
            _resident((1, d)),
            _resident((d, 2 * w)),
            _resident((1, 2 * w)),
            _resident((conv_width, w)),
            _resident((1, w)),
            _resident((N_LRU_BLOCKS, blk, 2 * blk)),
            _resident((2, w)),
            _resident((1, w)),
            _resident((w, d)),
            _resident((1, d)),
        ],
        out_specs=pl.BlockSpec((1, ts, d), lambda b, s: (b, s, 0)),
        out_shape=jax.ShapeDtypeStruct((bsz, seq, d), F32),
        scratch_shapes=[pltpu.VMEM((V7X_SUBLANES, w), F32), pltpu.VMEM((V7X_SUBLANES, w), F32)],
        compiler_params=pltpu.CompilerParams(
            dimension_semantics=("parallel", "arbitrary"), vmem_limit_bytes=_vmem_limit(est)),
        name="lru",
    )(x, row(g), w_in.astype(BF16), row(b_in), conv_w, row(conv_b), gw, gate_b, row(lam),
      w_out.astype(BF16), row(b_out))


def _ffn_kernel(*refs, has_attn, final, fc):
    refs = list(refs)
    x_ref = refs.pop(0)
    o_ref, wo_ref = (refs.pop(0), refs.pop(0)) if has_attn else (None, None)
    g_ref, win_ref, wout_ref = refs.pop(0), refs.pop(0), refs.pop(0)
    fg_ref = refs.pop(0) if final else None
    out_ref = refs.pop(0)

    h = x_ref[0]
    if has_attn:
        h = h + _dot_tn(o_ref[0], wo_ref[...])
    hn = _rms_norm(h, g_ref[...]).astype(BF16)
    acc = h
    d_ff = wout_ref.shape[0]

    def gate_up(c):
        return (_dot(hn, win_ref[:, c * fc:(c + 1) * fc]),
                _dot(hn, win_ref[:, d_ff + c * fc:d_ff + (c + 1) * fc]))

    nxt = gate_up(0)
    for c in range(d_ff // fc):
        gate, up = nxt
        if (c + 1) * fc < d_ff:
            nxt = gate_up(c + 1)
        act = (gate * jax.nn.sigmoid(gate) * up).astype(BF16)
        acc = acc + _dot(act, wout_ref[c * fc:(c + 1) * fc, :])
    if final:
        acc = _rms_norm(acc, fg_ref[...])
    out_ref[0] = acc


def _ffn_block(x, g, w_in, w_out, *, attn=None, final_g=None, tm=1024, fc=256):
    bsz, seq, d = x.shape
    d_ff = w_out.shape[0]
    assert d_ff % fc == 0 and fc % V7X_LANES == 0
    win = w_in.astype(BF16)
    tile = pl.BlockSpec((1, tm, d), lambda b, s: (b, s, 0))
    args, specs = [x], [tile]
    est = 2 * 2 * tm * d * 4 + (win.size + w_out.size) * 2 + 6 * tm * d * 4
    if attn is not None:
        o, w_o = attn
        args += [o, w_o.astype(BF16)]
        specs += [pl.BlockSpec((1, o.shape[1], tm), lambda b, s: (b, 0, s)), _resident(w_o.shape)]
        est += 2 * tm * o.shape[1] * 2 + w_o.size * 2
    args += [g.reshape(1, d), win, w_out.astype(BF16)]
    specs += [_resident((1, d)), _resident(win.shape), _resident(w_out.shape)]
    if final_g is not None:
        args.append(final_g.reshape(1, d))
        specs.append(_resident((1, d)))
    return pl.pallas_call(
        functools.partial(_ffn_kernel, has_attn=attn is not None, final=final_g is not None, fc=fc),
        grid=(bsz, seq // tm),
        in_specs=specs,
        out_specs=tile,
        out_shape=jax.ShapeDtypeStruct((bsz, seq, d), F32),
        compiler_params=pltpu.CompilerParams(
            dimension_semantics=("parallel", "parallel"), vmem_limit_bytes=_vmem_limit(est)),
        name="ffn",
    )(*args)


def _qkv_kernel(x_ref, g_ref, wq_ref, wk_ref, wvt_ref, q_ref, k_ref, vt_ref, *, q_scale):
    ts = x_ref.shape[1]
    hn = _rms_norm(x_ref[0], g_ref[...]).astype(BF16)
    q_ref[0] = (_dot(hn, wq_ref[...]) * q_scale).astype(BF16)
    hp = _to_strided_order(hn)
    k_ref[0] = _dot(hp, wk_ref[...]).astype(BF16)
    vt = _dot_nt(wvt_ref[...], hp).astype(BF16)
    for j in range(ts // KEY_TILE):
        vt_ref[0, j] = vt[:, j * KEY_TILE:(j + 1) * KEY_TILE]


def _qkv_proj(x, g, w_qkv, *, ts=512):
    bsz, seq, d = x.shape
    hd = w_qkv.shape[1] // 3
    wq = w_qkv[:, :hd].astype(BF16)
    wk = w_qkv[:, hd:2 * hd].astype(BF16)
    wvt = w_qkv[:, 2 * hd:].T.astype(BF16)
    q_scale = 0.5 / math.sqrt(HEAD_DIM)
    est = 2 * ts * d * 4 + 3 * d * hd * 2 + 2 * 3 * ts * hd * 2 + 6 * ts * d * 4
    return pl.pallas_call(
        functools.partial(_qkv_kernel, q_scale=q_scale),
        grid=(bsz, seq // ts),
        in_specs=[
            pl.BlockSpec((1, ts, d), lambda b, s: (b, s, 0)),
            _resident((1, d)), _resident((d, hd)), _resident((d, hd)), _resident((hd, d)),
        ],
        out_specs=[
            pl.BlockSpec((1, ts, hd), lambda b, s: (b, s, 0)),
            pl.BlockSpec((1, ts, hd), lambda b, s: (b, s, 0)),
            pl.BlockSpec((1, ts // KEY_TILE, hd, KEY_TILE), lambda b, s: (b, s, 0, 0)),
        ],
        out_shape=[
            jax.ShapeDtypeStruct((bsz, seq, hd), BF16),
            jax.ShapeDtypeStruct((bsz, seq, hd), BF16),
            jax.ShapeDtypeStruct((bsz, seq // KEY_TILE, hd, KEY_TILE), BF16),
        ],
        compiler_params=pltpu.CompilerParams(
            dimension_semantics=("parallel", "parallel"), vmem_limit_bytes=_vmem_limit(est)),
        name="qkv",
    )(x, g.reshape(1, d), wq, wk, wvt)


def _shift_up(x, k):
    sub = lax.broadcasted_iota(jnp.int32, x.shape, 0)
    return jnp.where(sub < V7X_SUBLANES - k, pltpu.roll(x, V7X_SUBLANES - k, 0), 1.0)


def _group_weights(s_g, c, valid):
    shape3 = (KEY_VREGS, V7X_SUBLANES, V7X_LANES)
    beta = (0.5 + 0.5 * jnp.tanh(s_g)).reshape(shape3)
    if valid is not None:
        beta = jnp.where(valid, beta, 0.0)
    run = jnp.ones(shape3[1:], F32)
    part = [None] * KEY_VREGS
    for v in range(KEY_VREGS - 1, -1, -1):
        part[v] = beta[v] * run
        run = run - part[v]
    inc = run
    for k in (1, 2, 4):
        inc = inc * _shift_up(inc, k)
    later = _shift_up(inc, 1) * c
    c = jnp.broadcast_to(inc[0:1, :], inc.shape) * c
    wgt = jnp.concatenate([part[v] * later for v in range(KEY_VREGS)], axis=0)
    return wgt.astype(BF16), c


def _attn_band(kt0, n_tiles, n_diag, skew, *, tri, q_cat, k_ref, vt_ref, carry):
    n_groups = len(carry[0][1])
    nq = n_groups * V7X_LANES
    scores = {}
    for p in range(n_tiles // SCORE_TILES - 1, -1, -1):
        row0 = pl.multiple_of((kt0 + p * SCORE_TILES) * KEY_TILE, KEY_TILE)
        rows = k_ref[0, pl.ds(row0, SCORE_TILES * KEY_TILE), :]
        for lb, q_lb in enumerate(q_cat):
            scores[p, lb] = _dot_nt(rows[:, lb * V7X_LANES:(lb + 1) * V7X_LANES], q_lb)

    carry = [(acc, list(cs)) for acc, cs in carry]
    finished = []
    for d in range(n_tiles - 1, -1, -1):
        p, r = divmod(d, SCORE_TILES)
        j = d - (n_tiles - n_diag)
        for hi, (acc, cs) in enumerate(carry):
            lb, hh = divmod(hi, HEADS_PER_BLOCK)
            w_groups = []
            for g in range(n_groups):
                if g < j or (skew and d < g):
                    w_groups.append(jnp.zeros((KEY_TILE, V7X_LANES), BF16))
                    continue
                col0 = hh * nq + g * V7X_LANES
                s_g = scores[p, lb][r * KEY_TILE:(r + 1) * KEY_TILE, col0:col0 + V7X_LANES]
                if len(finished) >= GROUP_LAG:
                    bits = lax.bitcast_convert_type(finished[-GROUP_LAG], jnp.uint32)
                    zero = lax.bitcast_convert_type(
                        lax.shift_right_logical(lax.shift_right_logical(bits, jnp.uint32(16)), jnp.uint32(16)), F32)
                    s_g = (s_g.reshape(KEY_VREGS, V7X_SUBLANES, V7X_LANES) + zero).reshape(KEY_TILE, V7X_LANES)
                w_g, cs[g] = _group_weights(s_g, cs[g], tri if g == j else None)
                finished.append(cs[g])
                w_groups.append(w_g)
            v_blk = vt_ref[0, kt0 + d, hi * HEAD_DIM:(hi + 1) * HEAD_DIM, :]
            carry[hi] = (acc + _dot(v_blk, jnp.concatenate(w_groups, axis=1)), cs)
    return tuple((acc, tuple(cs)) for acc, cs in carry)


def _attn_tail(kt, *, q_cat, k_ref, vt_ref, carry):
    n_groups = len(carry[0][1])
    nq = n_groups * V7X_LANES
    vreg = lax.broadcasted_iota(jnp.int32, (KEY_VREGS, V7X_SUBLANES, V7X_LANES), 0)
    blank = jnp.zeros((KEY_TILE, V7X_LANES), BF16)
    carry = [(acc, list(cs)) for acc, cs in carry]
    for g in range(n_groups):
        tile = jnp.maximum(kt + g, 0)
        on = vreg >= jnp.where(kt + g >= 0, 0, KEY_VREGS)
        rows = k_ref[0, pl.ds(pl.multiple_of(tile * KEY_TILE, KEY_TILE), KEY_TILE), :]
        for lb, q_lb in enumerate(q_cat):
            s_t = _dot_nt(rows[:, lb * V7X_LANES:(lb + 1) * V7X_LANES], q_lb)
            for hh in range(HEADS_PER_BLOCK):
                hi = lb * HEADS_PER_BLOCK + hh
                acc, cs = carry[hi]
                col0 = hh * nq + g * V7X_LANES
                w_g, cs[g] = _group_weights(s_t[:, col0:col0 + V7X_LANES], cs[g], on)
                wgt = jnp.concatenate([w_g if gg == g else blank for gg in range(n_groups)], axis=1)
                v_blk = vt_ref[0, tile, hi * HEAD_DIM:(hi + 1) * HEAD_DIM, :]
                carry[hi] = (acc + _dot(v_blk, wgt), cs)
    return tuple((acc, tuple(cs)) for acc, cs in carry)


def _attn_kernel(q_ref, k_ref, vt_ref, o_ref):
    nq = Q_TILE
    n_blocks = q_ref.shape[2] // V7X_LANES
    n_diag = nq // KEY_TILE
    step_idx = pl.program_id(2)
    shape3 = (KEY_VREGS, V7X_SUBLANES, V7X_LANES)
    tri = (lax.broadcasted_iota(jnp.int32, shape3, 1) * KEY_VREGS + lax.broadcasted_iota(jnp.int32, shape3, 0)
           < lax.broadcasted_iota(jnp.int32, shape3, 2))
    lane = lax.broadcasted_iota(jnp.int32, (nq, V7X_LANES), 1)

    def head_queries(sub):
        q_cat = []
        for lb in range(n_blocks):
            q = q_ref[0, sub * nq:(sub + 1) * nq, lb * V7X_LANES:(lb + 1) * V7X_LANES]
            q_cat.append(jnp.concatenate(
                [jnp.where(lane // HEAD_DIM == hh, q, jnp.zeros_like(q)) for hh in range(HEADS_PER_BLOCK)], axis=0))
        return q_cat

    def diag0(sub):
        return (step_idx * Q_TILES_PER_STEP + sub) * n_diag

    fresh = tuple((jnp.zeros((HEAD_DIM, nq), F32),
                   tuple(jnp.ones((V7X_SUBLANES, V7X_LANES), F32) for _ in range(n_diag)))
                  for _ in range(n_blocks * HEADS_PER_BLOCK))

    def with_alive(cr):
        c_max = functools.reduce(jnp.maximum, [c for _, cs in cr for c in cs])
        return (jnp.max(c_max) > 0.0).astype(jnp.int32), cr

    def bands(at_start):
        out = []
        for sub in range(Q_TILES_PER_STEP):
            common = dict(tri=tri, q_cat=head_queries(sub), k_ref=k_ref, vt_ref=vt_ref, carry=fresh)
            if at_start and sub == 0:
                out.append(with_alive(_attn_band(0, n_diag, n_diag, False, **common)))
            else:
                out.append(with_alive(_attn_band(diag0(sub) - BAND_BELOW, BAND_BELOW + n_diag, n_diag, True,
                                                 **common)))
        return tuple(out)

    results = lax.cond(step_idx == 0, lambda: bands(True), lambda: bands(False))

    for sub, (live, carry) in enumerate(results):
        def more(state):
            kt, live, _ = state
            return jnp.logical_and(kt > -n_diag, live > 0)

        def step(state, sub=sub):
            kt, _, cr = state
            return (kt - 1,) + with_alive(
                _attn_tail(kt, q_cat=head_queries(sub), k_ref=k_ref, vt_ref=vt_ref, carry=cr))

        kt_first = jnp.where(diag0(sub) == 0, -n_diag, diag0(sub) - BAND_BELOW - 1)
        _, _, carry = lax.while_loop(more, step, (kt_first, live, carry))
        o_ref[0, :, sub * nq:(sub + 1) * nq] = jnp.concatenate([acc for acc, _ in carry], axis=0).astype(BF16)


def _attention(q, k, vt):
    bsz, seq, hd = q.shape
    n_kt = seq // KEY_TILE
    width = ATTN_LANE_BLOCKS * V7X_LANES
    assert KEY_TILE == V7X_LANES and hd % width == 0
    assert BAND_BELOW % SCORE_TILES == 0 and (Q_TILE // KEY_TILE) % SCORE_TILES == 0
    q_step = Q_TILE * Q_TILES_PER_STEP
    assert seq % q_step == 0
    est = (2 * (2 * q_step * width * 2 + seq * width * 2 * 2)
           + 16 * Q_TILES_PER_STEP * ATTN_LANE_BLOCKS * KEY_TILE * Q_TILE * 4)
    return pl.pallas_call(
        _attn_kernel,
        grid=(bsz, hd // width, seq // q_step),
        in_specs=[
            pl.BlockSpec((1, q_step, width), lambda b, h, i: (b, i, h)),
            pl.BlockSpec((1, seq, width), lambda b, h, i: (b, 0, h)),
            pl.BlockSpec((1, n_kt, width, KEY_TILE), lambda b, h, i: (b, 0, h, 0)),
        ],
        out_specs=pl.BlockSpec((1, width, q_step), lambda b, h, i: (b, h, i)),
        out_shape=jax.ShapeDtypeStruct((bsz, hd, seq), BF16),
        compiler_params=pltpu.CompilerParams(
            dimension_semantics=("parallel", "parallel", "arbitrary"), vmem_limit_bytes=_vmem_limit(est)),
        name="attn",
    )(q, k, vt)


def kernel(x, mix_norm, ffn_norm, final_norm, lru_w_in, lru_b_in, lru_conv_w, lru_conv_b, lru_gate_w, lru_gate_b,
           lru_lambda, lru_w_out, lru_b_out, attn_w_qkv, attn_w_o, ffn_w_in, ffn_w_out):
    depth = mix_norm.shape[0]
    assert x.shape[1] % 512 == 0 and x.shape[2] % V7X_LANES == 0
    h = x
    for layer in range(depth):
        j = layer // 2
        final_g = final_norm if layer == depth - 1 else None
        if layer % 2 == 0:
            h = _lru_block(h, mix_norm[layer], lru_w_in[j], lru_b_in[j], lru_conv_w[j], lru_conv_b[j],
                           lru_gate_w[j], lru_gate_b[j], lru_lambda[j], lru_w_out[j], lru_b_out[j])
            attn = None
        else:
            q, k, vt = _qkv_proj(h, mix_norm[layer], attn_w_qkv[j])
            attn = (_attention(q, k, vt), attn_w_o[j])
        h = _ffn_block(h, ffn_norm[layer], ffn_w_in[layer], ffn_w_out[layer], attn=attn, final_g=final_g)
    return h
```
